```python
import math
import jax, jax.numpy as jnp
from jax import lax
import numpy as np

D_MODEL = 2048
BATCH = 4
SEQ = 2048
DEPTH = 2
DEC_BATCH = 32
DEC_SEQ = 8
PAST_LEN = 8192
PAGE_SIZE = 128

HEAD_DIM = 128
GDN_WIDTH = D_MODEL // 4
GDN_HEADS = GDN_WIDTH // HEAD_DIM
DIFF_WIDTH = D_MODEL // 2
DIFF_HEADS = DIFF_WIDTH // (2 * HEAD_DIM)
LRU_WIDTH = D_MODEL // 4
LRU_BLOCKS = 8
LRU_BLOCK_W = LRU_WIDTH // LRU_BLOCKS
MIX_WIDTH = GDN_WIDTH + DIFF_WIDTH + LRU_WIDTH
CONV_W = 4
GDN_CHUNK = 64
Q_BLOCK = 128
D_FF = 5632
ROPE_THETA = 10000.0
LRU_C = 8.0
EPS = 1e-6
N_MOD = 9
IN_SIZES = (3 * GDN_WIDTH, GDN_WIDTH, GDN_HEADS, GDN_HEADS, DIFF_WIDTH, DIFF_WIDTH, DIFF_WIDTH, LRU_WIDTH, LRU_WIDTH)
N_IN = 4 * GDN_WIDTH + 2 * GDN_HEADS + 3 * DIFF_WIDTH + 2 * LRU_WIDTH

kernel_name = 'hymba_gdn_diffattn_rglru_macaron_step'


def rmsnorm(x, g):
    x32 = x.astype(jnp.float32)
    y = x32 * lax.rsqrt(jnp.mean(x32 * x32, axis=-1, keepdims=True) + EPS)
    return (y * g.astype(jnp.float32)).astype(x.dtype)


def l2norm(x):
    x32 = x.astype(jnp.float32)
    return x32 * lax.rsqrt(jnp.sum(x32 * x32, axis=-1, keepdims=True) + EPS)


def modulate(h, shift, scale):
    return h * (1.0 + scale) + shift


def swiglu(h, w_gate, w_up, w_down):
    return (jax.nn.silu(h @ w_gate) * (h @ w_up)) @ w_down


def split_in(z):
    parts, start = [], 0
    for s in IN_SIZES:
        parts.append(z[..., start:start + s])
        start += s
    return parts


def causal_conv(x, buf, w, b=None):
    T = x.shape[1]
    xp = jnp.concatenate([buf.astype(x.dtype), x], axis=1)
    y = xp[:, 0:T] * w[0]
    for j in range(1, CONV_W):
        y = y + xp[:, j:j + T] * w[j]
    if b is not None:
        y = y + b
    return y, xp[:, T:]


def rope(x, pos):
    half = HEAD_DIM // 2
    inv = jnp.float32(ROPE_THETA) ** (-jnp.arange(half, dtype=jnp.float32) / half)
    ang = pos.astype(jnp.float32)[:, None] * inv[None, :]
    cos = jnp.cos(ang)[None, :, None, None, :]
    sin = jnp.sin(ang)[None, :, None, None, :]
    x32 = x.astype(jnp.float32)
    x1, x2 = x32[..., :half], x32[..., half:]
    return jnp.concatenate([x1 * cos - x2 * sin, x2 * cos + x1 * sin], axis=-1).astype(x.dtype)


def gated_delta(q, k, v, g, beta, s0):
    B, T, H, dk = q.shape
    dv = v.shape[-1]
    C = GDN_CHUNK
    pad = (-T) % C
    N = (T + pad) // C

    def chunks(a):
        a = jnp.pad(a, [(0, 0), (0, pad)] + [(0, 0)] * (a.ndim - 2))
        a = a.reshape((B, N, C) + a.shape[2:])
        return jnp.transpose(a, (1, 0, 3, 2) + tuple(range(4, a.ndim)))

    q, k, v, g, beta = chunks(q), chunks(k), chunks(v), chunks(g), chunks(beta)
    g = jnp.cumsum(g, axis=-1)
    idx = jnp.arange(C)
    incl = idx[:, None] >= idx[None, :]
    strict = idx[:, None] > idx[None, :]
    decay = jnp.exp(jnp.where(incl, g[..., :, None] - g[..., None, :], -jnp.inf))
    kb = k * beta[..., None]
    a_mat = jnp.where(strict, jnp.einsum('nbhik,nbhjk->nbhij', kb, k) * decay, 0.0) + jnp.eye(C, dtype=jnp.float32)
    rhs = jnp.concatenate([v * beta[..., None], kb * jnp.exp(g)[..., None]], axis=-1)
    sol = lax.linalg.triangular_solve(a_mat, rhs, left_side=True, lower=True, unit_diagonal=True)
    u, w = sol[..., :dv], sol[..., dv:]
    qk = jnp.einsum('nbhik,nbhjk->nbhij', q, k) * decay

    def step(s, inp):
        q_c, k_c, u_c, w_c, qk_c, g_c = inp
        v_new = u_c - jnp.einsum('bhck,bhkv->bhcv', w_c, s)
        o = (jnp.einsum('bhck,bhkv->bhcv', q_c * jnp.exp(g_c)[..., None], s)
             + jnp.einsum('bhij,bhjv->bhiv', qk_c, v_new))
        g_last = g_c[..., -1:]
        s = (s * jnp.exp(g_last)[..., None]
             + jnp.einsum('bhck,bhcv->bhkv', k_c * jnp.exp(g_last - g_c)[..., None], v_new))
        return s, o

    s, o = lax.scan(step, s0, (q, k, u, w, qk, g))
    o = jnp.transpose(o, (1, 0, 3, 2, 4)).reshape(B, N * C, H, dv)[:, :T]
    return o, s


def diff_pair(s, lam):
    p = jax.nn.softmax(s, axis=-1)
    return p[:, :, 0] - lam * p[:, :, 1]


def diff_attn_causal(q, k, v, lam):
    B, T, H, _, dh = q.shape
    blk = Q_BLOCK if T % Q_BLOCK == 0 else T
    nb = T // blk
    qb = jnp.transpose(q.reshape(B, nb, blk, H, 2, dh), (1, 0, 2, 3, 4, 5))
    kpos = jnp.arange(T)
    scale = dh ** -0.5

    def block(args):
        qi, i = args
        s = jnp.einsum('bqhcd,bkhcd->bhcqk', qi, k, preferred_element_type=jnp.float32) * scale
        qpos = i * blk + jnp.arange(blk)
        s = jnp.where(kpos[None, :] <= qpos[:, None], s, -jnp.inf)
        p = diff_pair(s, lam)
        return jnp.einsum('bhqk,bkhe->bqhe', p.astype(v.dtype), v)

    o = lax.map(block, (qb, jnp.arange(nb)))
    return jnp.transpose(o, (1, 0, 2, 3, 4)).reshape(B, T, H, 2 * dh)


def diff_attn_with_past(q, k, v, past_k, past_v, lam):
    B, T, H, _, dh = q.shape
    P = past_k.shape[1]
    scale = dh ** -0.5
    kp = past_k.reshape(B, P, H, 2, dh)
    s_past = jnp.einsum('bqhcd,bkhcd->bhcqk', q, kp, preferred_element_type=jnp.float32) * scale
    s_new = jnp.einsum('bqhcd,bkhcd->bhcqk', q, k, preferred_element_type=jnp.float32) * scale
    tri = jnp.arange(T)[:, None] >= jnp.arange(T)[None, :]
    s_new = jnp.where(tri, s_new, -jnp.inf)
    p = diff_pair(jnp.concatenate([s_past, s_new], axis=-1), lam)
    return (jnp.einsum('bhqk,bkhe->bqhe', p[..., :P].astype(v.dtype), past_v)
            + jnp.einsum('bhqk,bkhe->bqhe', p[..., P:].astype(v.dtype), v))


def rglru(xc, wa, ba, wx, bx, lam, h0):
    B, T, W = xc.shape
    x32 = xc.astype(jnp.float32)
    xb = x32.reshape(B, T, LRU_BLOCKS, LRU_BLOCK_W)
    r = jax.nn.sigmoid(jnp.einsum('btnd,nde->btne', xb, wa.astype(jnp.float32)).reshape(B, T, W) + ba.astype(jnp.float32))
    i = jax.nn.sigmoid(jnp.einsum('btnd,nde->btne', xb, wx.astype(jnp.float32)).reshape(B, T, W) + bx.astype(jnp.float32))
    log_a = -LRU_C * r * jax.nn.softplus(-lam.astype(jnp.float32))
    a = jnp.exp(log_a)
    b = jnp.sqrt(-jnp.expm1(2.0 * log_a)) * (i * x32)
    b = b.at[:, 0].add(a[:, 0] * h0.astype(jnp.float32))

    def combine(left, right):
        a_l, b_l = left
        a_r, b_r = right
        return a_l * a_r, a_r * b_l + b_r

    _, h = lax.associative_scan(combine, (a, b), axis=1)
    return h, h[:, -1]


def token_mixer(h, l, W, pos0, gdn_s0, gdn_buf0, lru_h0, lru_buf0, past_k, past_v):
    B, T, _ = h.shape
    z = h @ W['w_in'][l]
    gqkv, gz, gb, ga, dq, dk, dv, lx, lg = split_in(z)

    qkv, gdn_buf = causal_conv(gqkv, gdn_buf0, W['gdn_conv_w'][l])
    qkv = jax.nn.silu(qkv)
    gq = l2norm(qkv[..., :GDN_WIDTH].reshape(B, T, GDN_HEADS, HEAD_DIM)) * (HEAD_DIM ** -0.5)
    gk = l2norm(qkv[..., GDN_WIDTH:2 * GDN_WIDTH].reshape(B, T, GDN_HEADS, HEAD_DIM))
    gv = qkv[..., 2 * GDN_WIDTH:].reshape(B, T, GDN_HEADS, HEAD_DIM).astype(jnp.float32)
    beta = jax.nn.sigmoid(gb.astype(jnp.float32))
    g = -jnp.exp(W['gdn_a_log'][l].astype(jnp.float32)) * jax.nn.softplus(
        ga.astype(jnp.float32) + W['gdn_dt_bias'][l].astype(jnp.float32))
    o_gdn, s_gdn = gated_delta(gq, gk, gv, g, beta, gdn_s0.astype(jnp.float32))
    o_gdn = rmsnorm(o_gdn.astype(h.dtype), W['gdn_norm'][l]) * jax.nn.silu(gz.reshape(B, T, GDN_HEADS, HEAD_DIM))

    pos = pos0 + jnp.arange(T, dtype=jnp.int32)
    q = rope(dq.reshape(B, T, DIFF_HEADS, 2, HEAD_DIM), pos)
    k = rope(dk.reshape(B, T, DIFF_HEADS, 2, HEAD_DIM), pos)
    v = dv.reshape(B, T, DIFF_HEADS, 2 * HEAD_DIM)
    lam_init = 0.8 - 0.6 * math.exp(-0.3 * l)
    lam = (jnp.exp(jnp.sum(W['diff_lq1'][l].astype(jnp.float32) * W['diff_lk1'][l].astype(jnp.float32)))
           - jnp.exp(jnp.sum(W['diff_lq2'][l].astype(jnp.float32) * W['diff_lk2'][l].astype(jnp.float32))) + lam_init)
    if past_k is None:
        o_diff = diff_attn_causal(q, k, v, lam)
    else:
        o_diff = diff_attn_with_past(q, k, v, past_k, past_v, lam)
    o_diff = rmsnorm(o_diff, W['diff_subln'][l]) * (1.0 - lam_init)
    k_rows = k.reshape(B, T, DIFF_HEADS, 2 * HEAD_DIM)

    xc, lru_buf = causal_conv(lx, lru_buf0, W['lru_conv_w'][l], W['lru_conv_b'][l])
    h_lru, h_last = rglru(xc, W['lru_wa'][l], W['lru_ba'][l], W['lru_wx'][l], W['lru_bx'][l], W['lru_lambda'][l], lru_h0)
    o_lru = h_lru.astype(h.dtype) * jax.nn.gelu(lg)

    mix = jnp.concatenate([o_gdn.reshape(B, T, GDN_WIDTH), o_diff.reshape(B, T, DIFF_WIDTH), o_lru], axis=-1)
    y = mix @ W['w_out'][l]
    return y, (k_rows, v, s_gdn.astype(h.dtype), gdn_buf, h_last.astype(h.dtype), lru_buf)


def trunk(x, c, pos0, W, past):
    B = x.shape[0]
    new = [[] for _ in range(6)]
    cs = jax.nn.silu(c)
    for l in range(DEPTH):
        mod = cs @ W['ada_w'][l] + W['ada_b'][l]
        sh1, sc1, g1, sh2, sc2, g2, sh3, sc3, g3 = [m[:, None, :] for m in jnp.split(mod, N_MOD, axis=-1)]
        h = modulate(rmsnorm(x, W['norm_ffn1'][l]), sh1, sc1)
        x = x + 0.5 * g1 * swiglu(h, W['ffn1_w_gate'][l], W['ffn1_w_up'][l], W['ffn1_w_down'][l])
        h = modulate(rmsnorm(x, W['norm_mix'][l]), sh2, sc2)
        if past is None:
            gdn_s0 = jnp.zeros((B, GDN_HEADS, HEAD_DIM, HEAD_DIM), jnp.float32)
            gdn_buf0 = jnp.zeros((B, CONV_W - 1, 3 * GDN_WIDTH), x.dtype)
            lru_h0 = jnp.zeros((B, LRU_WIDTH), jnp.float32)
            lru_buf0 = jnp.zeros((B, CONV_W - 1, LRU_WIDTH), x.dtype)
            past_k = past_v = None
        else:
            gdn_s0 = past['state_gdn'][:, l]
            gdn_buf0 = past['state_gdn_conv'][:, l]
            lru_h0 = past['state_lru'][:, l]
            lru_buf0 = past['state_lru_conv'][:, l]
            past_k = past['cache_k'][past['page_table'], l].reshape(B, -1, DIFF_HEADS, 2 * HEAD_DIM)
            past_v = past['cache_v'][past['page_table'], l].reshape(B, -1, DIFF_HEADS, 2 * HEAD_DIM)
        y, st = token_mixer(h, l, W, pos0, gdn_s0, gdn_buf0, lru_h0, lru_buf0, past_k, past_v)
        x = x + g2 * y
        h = modulate(rmsnorm(x, W['norm_ffn2'][l]), sh3, sc3)
        x = x + 0.5 * g3 * swiglu(h, W['ffn2_w_gate'][l], W['ffn2_w_up'][l], W['ffn2_w_down'][l])
        for lst, s in zip(new, st):
            lst.append(s)
    y = rmsnorm(x, W['final_norm'])
    k_new, v_new, s_new, sb_new, h_new, hb_new = [jnp.stack(lst, axis=1) for lst in new]
    return y, k_new, v_new, s_new, sb_new, h_new, hb_new


def setup_inputs(seed: int = 0) -> dict:
    key = jax.random.key(seed)
    keys = iter(jax.random.split(key, 64))
    f32 = jnp.float32

    def normal(shape, scale):
        return jax.random.normal(next(keys), shape, f32) * scale

    def gain(shape):
        return 1.0 + normal(shape, 0.02)

    n_pages = PAST_LEN // PAGE_SIZE
    n_used = DEC_BATCH * n_pages
    n_pool = n_used + max(1, n_used // 4)
    inp = {}
    inp['x_prompt'] = normal((BATCH, SEQ, D_MODEL), 1.0)
    inp['x_sample'] = normal((DEC_BATCH, DEC_SEQ, D_MODEL), 1.0)
    inp['cache_k'] = normal((n_pool, DEPTH, PAGE_SIZE, DIFF_HEADS, 2 * HEAD_DIM), 1.0)
    inp['cache_v'] = normal((n_pool, DEPTH, PAGE_SIZE, DIFF_HEADS, 2 * HEAD_DIM), 1.0)
    inp['state_gdn'] = normal((DEC_BATCH, DEPTH, GDN_HEADS, HEAD_DIM, HEAD_DIM), 0.5)
    inp['state_gdn_conv'] = normal((DEC_BATCH, DEPTH, CONV_W - 1, 3 * GDN_WIDTH), 1.0)
    inp['state_lru'] = normal((DEC_BATCH, DEPTH, LRU_WIDTH), 0.5)
    inp['state_lru_conv'] = normal((DEC_BATCH, DEPTH, CONV_W - 1, LRU_WIDTH), 1.0)
    perm = jax.random.permutation(next(keys), n_pool)
    inp['page_table'] = perm[:n_used].reshape(DEC_BATCH, n_pages).astype(jnp.int32)
    inp['c_prompt'] = normal((BATCH, D_MODEL), 1.0)
    inp['c_sample'] = normal((DEC_BATCH, D_MODEL), 1.0)

    inp['ada_w'] = normal((DEPTH, D_MODEL, N_MOD * D_MODEL), 0.5 * D_MODEL ** -0.5)
    inp['ada_b'] = normal((DEPTH, N_MOD * D_MODEL), 0.02)
    inp['norm_ffn1'] = gain((DEPTH, D_MODEL))
    inp['ffn1_w_gate'] = normal((DEPTH, D_MODEL, D_FF), D_MODEL ** -0.5)
    inp['ffn1_w_up'] = normal((DEPTH, D_MODEL, D_FF), D_MODEL ** -0.5)
    inp['ffn1_w_down'] = normal((DEPTH, D_FF, D_MODEL), D_FF ** -0.5)
    inp['norm_mix'] = gain((DEPTH, D_MODEL))
    inp['w_in'] = normal((DEPTH, D_MODEL, N_IN), D_MODEL ** -0.5)
    inp['gdn_conv_w'] = normal((DEPTH, CONV_W, 3 * GDN_WIDTH), CONV_W ** -0.5)
    inp['gdn_a_log'] = jnp.log(jax.random.uniform(next(keys), (DEPTH, GDN_HEADS), f32, 1.0, 16.0))
    dt = jnp.exp(jax.random.uniform(next(keys), (DEPTH, GDN_HEADS), f32, math.log(1e-3), math.log(1e-1)))
    inp['gdn_dt_bias'] = dt + jnp.log(-jnp.expm1(-dt))
    inp['gdn_norm'] = gain((DEPTH, HEAD_DIM))
    inp['diff_lq1'] = normal((DEPTH, HEAD_DIM), 0.1)
    inp['diff_lk1'] = normal((DEPTH, HEAD_DIM), 0.1)
    inp['diff_lq2'] = normal((DEPTH, HEAD_DIM), 0.1)
    inp['diff_lk2'] = normal((DEPTH, HEAD_DIM), 0.1)
    inp['diff_subln'] = gain((DEPTH, 2 * HEAD_DIM))
    inp['lru_conv_w'] = normal((DEPTH, CONV_W, LRU_WIDTH), CONV_W ** -0.5)
    inp['lru_conv_b'] = normal((DEPTH, LRU_WIDTH), 0.02)
    inp['lru_wa'] = normal((DEPTH, LRU_BLOCKS, LRU_BLOCK_W, LRU_BLOCK_W), LRU_BLOCK_W ** -0.5)
    inp['lru_ba'] = normal((DEPTH, LRU_WIDTH), 0.02)
    inp['lru_wx'] = normal((DEPTH, LRU_BLOCKS, LRU_BLOCK_W, LRU_BLOCK_W), LRU_BLOCK_W ** -0.5)
    inp['lru_bx'] = normal((DEPTH, LRU_WIDTH), 0.02)
    a_c = jax.random.uniform(next(keys), (DEPTH, LRU_WIDTH), f32, 0.9, 0.999)
    s = a_c ** (1.0 / LRU_C)
    inp['lru_lambda'] = jnp.log(s) - jnp.log1p(-s)
    inp['w_out'] = normal((DEPTH, MIX_WIDTH, D_MODEL), MIX_WIDTH ** -0.5)
    inp['norm_ffn2'] = gain((DEPTH, D_MODEL))
    inp['ffn2_w_gate'] = normal((DEPTH, D_MODEL, D_FF), D_MODEL ** -0.5)
    inp['ffn2_w_up'] = normal((DEPTH, D_MODEL, D_FF), D_MODEL ** -0.5)
    inp['ffn2_w_down'] = normal((DEPTH, D_FF, D_MODEL), D_FF ** -0.5)
    inp['final_norm'] = gain((D_MODEL,))
    return inp


def reference(x_prompt, x_sample, cache_k, cache_v, state_gdn, state_gdn_conv, state_lru, state_lru_conv,
              page_table, c_prompt, c_sample, ada_w, ada_b, norm_ffn1, ffn1_w_gate, ffn1_w_up, ffn1_w_down,
              norm_mix, w_in, gdn_conv_w, gdn_a_log, gdn_dt_bias, gdn_norm, diff_lq1, diff_lk1, diff_lq2,
              diff_lk2, diff_subln, lru_conv_w, lru_conv_b, lru_wa, lru_ba, lru_wx, lru_bx, lru_lambda, w_out,
              norm_ffn2, ffn2_w_gate, ffn2_w_up, ffn2_w_down, final_norm):
    W = dict(ada_w=ada_w, ada_b=ada_b, norm_ffn1=norm_ffn1, ffn1_w_gate=ffn1_w_gate, ffn1_w_up=ffn1_w_up,
             ffn1_w_down=ffn1_w_down, norm_mix=norm_mix, w_in=w_in, gdn_conv_w=gdn_conv_w, gdn_a_log=gdn_a_log,
             gdn_dt_bias=gdn_dt_bias, gdn_norm=gdn_norm, diff_lq1=diff_lq1, diff_lk1=diff_lk1, diff_lq2=diff_lq2,
             diff_lk2=diff_lk2, diff_subln=diff_subln, lru_conv_w=lru_conv_w, lru_conv_b=lru_conv_b, lru_wa=lru_wa,
             lru_ba=lru_ba, lru_wx=lru_wx, lru_bx=lru_bx, lru_lambda=lru_lambda, w_out=w_out, norm_ffn2=norm_ffn2,
             ffn2_w_gate=ffn2_w_gate, ffn2_w_up=ffn2_w_up, ffn2_w_down=ffn2_w_down, final_norm=final_norm)
    y_prompt, k_p, v_p, gdn_p, gdn_conv_p, lru_p, lru_conv_p = trunk(x_prompt, c_prompt, 0, W, None)
    past = dict(cache_k=cache_k, cache_v=cache_v, page_table=page_table, state_gdn=state_gdn,
                state_gdn_conv=state_gdn_conv, state_lru=state_lru, state_lru_conv=state_lru_conv)
    y_sample, k_s, v_s, gdn_s, gdn_conv_s, lru_s, lru_conv_s = trunk(x_sample, c_sample, PAST_LEN, W, past)
    return (y_prompt, y_sample, k_p, v_p, gdn_p, gdn_conv_p, lru_p, lru_conv_p,
            k_s, v_s, gdn_s, gdn_conv_s, lru_s, lru_conv_s)
```

```python
import functools
import math

import jax
import jax.numpy as jnp
from jax import lax
from jax.experimental import pallas as pl
from jax.experimental.pallas import tpu as pltpu

F32 = jnp.float32
BF16 = jnp.bfloat16
SDS = jax.ShapeDtypeStruct

EPS = 1e-6
HEAD_DIM = 128
CONV_W = 4
ROPE_THETA = 10000.0
LRU_C = 8.0
N_MOD = 9
LRU_BLOCKS = 8
SUBLANES = 8
LANES = 128
VMEM_LIMIT = 60 * 1024 * 1024
GDN_CHUNK = 128
PAGES_PER_STEP = 8
ROW_TILE = 768


def _cp(sem):
    return pltpu.CompilerParams(dimension_semantics=sem, vmem_limit_bytes=VMEM_LIMIT)


def _dot(a, b):
    return jnp.dot(a, b, preferred_element_type=F32)


def _dot_nt(a, b):
    return lax.dot_general(a, b, (((1,), (1,)), ((), ())), preferred_element_type=F32)


def _silu(x):
    return x * jax.nn.sigmoid(x)


def _expand8(v8, tm):
    return v8[:, None, :]


def _norm_mod(x, gain, shift8, scale8):
    tm, d = x.shape
    y = x * lax.rsqrt(jnp.mean(x * x, axis=-1, keepdims=True) + EPS) * gain
    y3 = y.reshape(tm // SUBLANES, SUBLANES, d)
    y3 = y3 * (1.0 + scale8[:, None, :]) + shift8[:, None, :]
    return y3.reshape(tm, d)


def _gated_residual(x, y, gate8, half):
    tm, d = x.shape
    g = gate8 * 0.5 if half else gate8
    y3 = y.reshape(tm // SUBLANES, SUBLANES, d) * g[:, None, :]
    return x + y3.reshape(tm, d)


def _ada_kernel(c_ref, w_ref, b_ref, o_ref):
    c = c_ref[...]
    o_ref[0] = _dot(_silu(c).astype(BF16), w_ref[0].astype(BF16)) + b_ref[0]


def _ada_call(c_all, ada_w, ada_b):
    n_layers, d, n = ada_w.shape
    mp = c_all.shape[0]
    tn = 1024
    return pl.pallas_call(
        _ada_kernel,
        out_shape=SDS((n_layers, mp, n), F32),
        grid=(n_layers, n // tn),
        in_specs=[pl.BlockSpec((mp, d), lambda l, j: (0, 0)),
                  pl.BlockSpec((1, d, tn), lambda l, j: (l, 0, j)),
                  pl.BlockSpec((1, 1, tn), lambda l, j: (l, 0, j))],
        out_specs=pl.BlockSpec((1, mp, tn), lambda l, j: (l, 0, j)),
        compiler_params=_cp(("arbitrary", "arbitrary")),
        name="ada_proj",
    )(c_all, ada_w, ada_b.reshape(n_layers, 1, n))


def _ffn_kernel(x_ref, gain_ref, sh_ref, sc_ref, gt_ref, wg_ref, wu_ref, wd_ref, o_ref, h_ref, *, n_j, col_chunk):
    j = pl.program_id(1)

    @pl.when(j == 0)
    def _():
        h_ref[...] = _norm_mod(x_ref[...], gain_ref[...], sh_ref[...], sc_ref[...]).astype(BF16)
        o_ref[...] = jnp.zeros_like(o_ref)

    h = h_ref[...]
    g = _dot(h, wg_ref[...])
    u = _dot(h, wu_ref[...])
    a = (_silu(g) * u).astype(BF16)
    d = o_ref.shape[1]
    for c in range(0, d, col_chunk):
        o_ref[:, c:c + col_chunk] += _dot(a, wd_ref[:, c:c + col_chunk])

    @pl.when(j == n_j - 1)
    def _():
        o_ref[...] = _gated_residual(x_ref[...], o_ref[...], gt_ref[...], half=True)


def _ffn_call(x, gain, mod8, k_shift, wg, wu, wd, *, tm, tf):
    m, d = x.shape
    f = wg.shape[1]
    n_j = f // tf
    t8 = tm // SUBLANES
    kern = functools.partial(_ffn_kernel, n_j=n_j, col_chunk=512)
    return pl.pallas_call(
        kern,
        out_shape=SDS((m, d), F32),
        grid=(m // tm, n_j),
        in_specs=[pl.BlockSpec((tm, d), lambda i, j: (i, 0)),
                  pl.BlockSpec((1, d), lambda i, j: (0, 0)),
                  pl.BlockSpec((t8, d), lambda i, j: (i, k_shift)),
                  pl.BlockSpec((t8, d), lambda i, j: (i, k_shift + 1)),
                  pl.BlockSpec((t8, d), lambda i, j: (i, k_shift + 2)),
                  pl.BlockSpec((d, tf), lambda i, j: (0, j)),
                  pl.BlockSpec((d, tf), lambda i, j: (0, j)),
                  pl.BlockSpec((tf, d), lambda i, j: (j, 0))],
        out_specs=pl.BlockSpec((tm, d), lambda i, j: (i, 0)),
        scratch_shapes=[pltpu.VMEM((tm, d), BF16)],
        compiler_params=_cp(("arbitrary", "arbitrary")),
        name="ffn_swiglu",
    )(x, gain, mod8, mod8, mod8, wg, wu, wd)


def _proj_in_kernel(x_ref, gain_ref, sh_ref, sc_ref, w_ref, o_ref, h_ref):
    @pl.when(pl.program_id(1) == 0)
    def _():
        h_ref[...] = _norm_mod(x_ref[...], gain_ref[...], sh_ref[...], sc_ref[...]).astype(BF16)

    o_ref[...] = _dot(h_ref[...], w_ref[...])


def _proj_in_call(x, gain, mod8, k_shift, w, *, tm, tn):
    m, d = x.shape
    n = w.shape[1]
    t8 = tm // SUBLANES
    return pl.pallas_call(
        _proj_in_kernel,
        out_shape=SDS((m, n), F32),
        grid=(m // tm, n // tn),
        in_specs=[pl.BlockSpec((tm, d), lambda i, j: (i, 0)),
                  pl.BlockSpec((1, d), lambda i, j: (0, 0)),
                  pl.BlockSpec((t8, d), lambda i, j: (i, k_shift)),
                  pl.BlockSpec((t8, d), lambda i, j: (i, k_shift + 1)),
                  pl.BlockSpec((d, tn), lambda i, j: (0, j))],
        out_specs=pl.BlockSpec((tm, tn), lambda i, j: (i, j)),
        scratch_shapes=[pltpu.VMEM((tm, d), BF16)],
        compiler_params=_cp(("arbitrary", "arbitrary")),
        name="proj_in",
    )(x, gain, mod8, mod8, w)


def _proj_out_kernel(a_ref, w_ref, x_ref, gt_ref, o_ref):
    o_ref[...] = _gated_residual(x_ref[...], _dot(a_ref[...], w_ref[...]), gt_ref[...], half=False)


def _proj_out_call(a, w, x, mod8, k_gate, *, tm, tn):
    m, k = a.shape
    n = w.shape[1]
    t8 = tm // SUBLANES
    gate_blk = (k_gate * x.shape[1]) // tn
    return pl.pallas_call(
        _proj_out_kernel,
        out_shape=SDS((m, n), F32),
        grid=(m // tm, n // tn),
        in_specs=[pl.BlockSpec((tm, k), lambda i, j: (i, 0)),
                  pl.BlockSpec((k, tn), lambda i, j: (0, j)),
                  pl.BlockSpec((tm, tn), lambda i, j: (i, j)),
                  pl.BlockSpec((t8, tn), lambda i, j: (i, gate_blk + j))],
        out_specs=pl.BlockSpec((tm, tn), lambda i, j: (i, j)),
        compiler_params=_cp(("arbitrary", "arbitrary")),
        name="proj_out",
    )(a, w, x, mod8)


def _final_norm_kernel(x_ref, g_ref, o_ref):
    x = x_ref[...]
    o_ref[...] = x * lax.rsqrt(jnp.mean(x * x, axis=-1, keepdims=True) + EPS) * g_ref[...]


def _final_norm_call(x, gain, row0, rows, tm):
    d = x.shape[1]
    blk0 = row0 // tm
    return pl.pallas_call(
        _final_norm_kernel,
        out_shape=SDS((rows, d), F32),
        grid=(rows // tm,),
        in_specs=[pl.BlockSpec((tm, d), lambda i: (blk0 + i, 0)),
                  pl.BlockSpec((1, d), lambda i: (0, 0))],
        out_specs=pl.BlockSpec((tm, d), lambda i: (i, 0)),
        compiler_params=_cp(("arbitrary",)),
        name="final_norm",
    )(x, gain)


def _rope_kernel(q_ref, k_ref, qo_ref, ko_ref, *, tq, seq, pos0):
    i = pl.program_id(0)
    half = HEAD_DIM // 2
    row = i * tq + lax.broadcasted_iota(jnp.int32, (tq, HEAD_DIM), 0)
    lane = lax.broadcasted_iota(jnp.int32, (tq, HEAD_DIM), 1)
    pos = (pos0 + (row & (seq - 1))).astype(F32)
    freq = (lane & (half - 1)).astype(F32)
    inv = jnp.float32(ROPE_THETA) ** (-freq / half)
    ang = pos * inv
    cos = jnp.cos(ang)
    sin = jnp.where(lane < half, -jnp.sin(ang), jnp.sin(ang))
    for src, dst in ((q_ref, qo_ref), (k_ref, ko_ref)):
        for g in range(src.shape[1] // HEAD_DIM):
            x = src[:, g * HEAD_DIM:(g + 1) * HEAD_DIM]
            dst[:, g * HEAD_DIM:(g + 1) * HEAD_DIM] = x * cos + pltpu.roll(x, half, 1) * sin


def _rope_call(z, row0, rows, seq, pos0, col_q, col_k, width, tq):
    assert seq & (seq - 1) == 0 and rows % tq == 0 and row0 % tq == 0
    blk0 = row0 // tq
    kern = functools.partial(_rope_kernel, tq=tq, seq=seq, pos0=pos0)
    return pl.pallas_call(
        kern,
        out_shape=(SDS((rows, width), F32), SDS((rows, width), F32)),
        grid=(rows // tq,),
        in_specs=[pl.BlockSpec((tq, width), lambda i: (blk0 + i, col_q // width)),
                  pl.BlockSpec((tq, width), lambda i: (blk0 + i, col_k // width))],
        out_specs=(pl.BlockSpec((tq, width), lambda i: (i, 0)),
                   pl.BlockSpec((tq, width), lambda i: (i, 0))),
        compiler_params=_cp(("arbitrary",)),
        name="rope_qk",
    )(z, z)


def _diff_lambda(lam_ref, lam_init):
    v = lam_ref[...]
    s1 = jnp.sum(v[0:1] * v[1:2], axis=-1, keepdims=True)
    s2 = jnp.sum(v[2:3] * v[3:4], axis=-1, keepdims=True)
    return jnp.exp(s1) - jnp.exp(s2) + lam_init


def _diff_finish(o1, o2, lam, sub, lam_init):
    o = o1 - lam * o2
    o = o * lax.rsqrt(jnp.mean(o * o, axis=-1, keepdims=True) + EPS) * sub
    return o * (1.0 - lam_init)


def _flash_kernel(lam_ref, sub_ref, q_ref, k_ref, v_ref, o_ref, kb_ref, vb_ref, m_ref, l_ref, acc_ref,
                  *, tq, lam_init):
    i = pl.program_id(2)
    dh = HEAD_DIM
    scale = dh ** -0.5

    @pl.when(i == 0)
    def _():
        kb_ref[...] = k_ref[...].astype(BF16)
        vb_ref[...] = v_ref[...].astype(BF16)

    q = q_ref[...]
    qs = (q[:, :dh].astype(BF16), q[:, dh:].astype(BF16))
    m_ref[...] = jnp.full(m_ref.shape, -jnp.inf, F32)
    l_ref[...] = jnp.zeros(l_ref.shape, F32)
    acc_ref[...] = jnp.zeros(acc_ref.shape, F32)

    def block(j, masked):
        start = pl.multiple_of(j * tq, tq)
        kblk = kb_ref[pl.ds(start, tq), :]
        vblk = vb_ref[pl.ds(start, tq), :]
        for c in range(2):
            s = _dot_nt(qs[c], kblk[:, c * dh:(c + 1) * dh]) * scale
            if masked:
                r = lax.broadcasted_iota(jnp.int32, s.shape, 0)
                cc = lax.broadcasted_iota(jnp.int32, s.shape, 1)
                s = jnp.where(cc <= r, s, -jnp.inf)
            m_prev = m_ref[c]
            m_new = jnp.maximum(m_prev, jnp.max(s, axis=-1, keepdims=True))
            alpha = jnp.exp(m_prev - m_new)
            p = jnp.exp(s - m_new)
            l_ref[c] = alpha * l_ref[c] + jnp.sum(p, axis=-1, keepdims=True)
            acc_ref[c] = alpha * acc_ref[c] + _dot(p.astype(BF16), vblk)
            m_ref[c] = m_new

    def body(j, carry):
        block(j, False)
        return carry

    lax.fori_loop(0, i, body, 0)
    block(i, True)

    lam = _diff_lambda(lam_ref, lam_init)
    o1 = acc_ref[0] / l_ref[0]
    o2 = acc_ref[1] / l_ref[1]
    o_ref[...] = _diff_finish(o1, o2, lam, sub_ref[...], lam_init).astype(o_ref.dtype)


def _flash_call(lam_vecs, sub, q_rot, k_rot, z, col_v, batch, seq, heads, lam_init, tq):
    w = 2 * HEAD_DIM
    nq = seq // tq
    kern = functools.partial(_flash_kernel, tq=tq, lam_init=lam_init)
    return pl.pallas_call(
        kern,
        out_shape=SDS((batch * seq, heads * w), BF16),
        grid=(batch, heads, nq),
        in_specs=[pl.BlockSpec((4, HEAD_DIM), lambda b, h, i: (0, 0)),
                  pl.BlockSpec((1, w), lambda b, h, i: (0, 0)),
                  pl.BlockSpec((tq, w), lambda b, h, i: (b * nq + i, h)),
                  pl.BlockSpec((seq, w), lambda b, h, i: (b, h)),
                  pl.BlockSpec((seq, w), lambda b, h, i: (b, col_v // w + h))],
        out_specs=pl.BlockSpec((tq, w), lambda b, h, i: (b * nq + i, h)),
        scratch_shapes=[pltpu.VMEM((seq, w), BF16), pltpu.VMEM((seq, w), BF16),
                        pltpu.VMEM((2, tq, 1), F32), pltpu.VMEM((2, tq, 1), F32), pltpu.VMEM((2, tq, w), F32)],
        compiler_params=_cp(("arbitrary", "arbitrary", "arbitrary")),
        name="diff_flash_prompt",
    )(lam_vecs, sub, q_rot, k_rot, z)


def _dec_attn_kernel(pt_ref, lam_ref, sub_ref, q_ref, kn_ref, vn_ref, *rest, n_steps, heads, tdec, lam_init):
    npg = PAGES_PER_STEP
    k_pages = rest[:npg]
    v_pages = rest[npg:2 * npg]
    o_ref, m_ref, l_ref, acc_ref = rest[2 * npg:]
    s_id = pl.program_id(1)
    dh = HEAD_DIM
    w = 2 * dh
    scale = dh ** -0.5

    @pl.when(s_id == 0)
    def _():
        m_ref[...] = jnp.full(m_ref.shape, -jnp.inf, F32)
        l_ref[...] = jnp.zeros(l_ref.shape, F32)
        acc_ref[...] = jnp.zeros(acc_ref.shape, F32)

    q = q_ref[...]
    lane = lax.broadcasted_iota(jnp.int32, (tdec, w), 1)

    def q_pair(h):
        qh = q[:, h * w:(h + 1) * w]
        return jnp.concatenate([jnp.where(lane < dh, qh, 0.0), jnp.where(lane >= dh, qh, 0.0)], axis=0).astype(BF16)

    def update(h, qp, kblk, vblk, mask):
        s = _dot_nt(qp, kblk.astype(BF16)) * scale
        if mask is not None:
            s = jnp.where(mask, s, -jnp.inf)
        m_prev = m_ref[h]
        m_new = jnp.maximum(m_prev, jnp.max(s, axis=-1, keepdims=True))
        alpha = jnp.exp(m_prev - m_new)
        p = jnp.exp(s - m_new)
        l_ref[h] = alpha * l_ref[h] + jnp.sum(p, axis=-1, keepdims=True)
        acc_ref[h] = alpha * acc_ref[h] + _dot(p.astype(BF16), vblk.astype(BF16))
        m_ref[h] = m_new

    qps = [q_pair(h) for h in range(heads)]
    for pg in range(npg):
        for h in range(heads):
            update(h, qps[h], k_pages[pg][:, h * w:(h + 1) * w], v_pages[pg][:, h * w:(h + 1) * w], None)

    @pl.when(s_id == n_steps - 1)
    def _():
        nk = kn_ref.shape[0]
        r = lax.broadcasted_iota(jnp.int32, (2 * tdec, nk), 0)
        cc = lax.broadcasted_iota(jnp.int32, (2 * tdec, nk), 1)
        mask = cc <= (r & (tdec - 1))
        lam = _diff_lambda(lam_ref, lam_init)
        for h in range(heads):
            update(h, qps[h], kn_ref[:, h * w:(h + 1) * w], vn_ref[:, h * w:(h + 1) * w], mask)
            o = acc_ref[h] / l_ref[h]
            o_ref[:, h * w:(h + 1) * w] = _diff_finish(o[:tdec], o[tdec:], lam, sub_ref[...], lam_init).astype(o_ref.dtype)


def _dec_attn_call(page_table, lam_vecs, sub, q_rot, k_new_pad, v_new_pad, cache_k, cache_v, layer, heads, tdec, lam_init):
    batch, n_pages = page_table.shape
    page = cache_k.shape[2]
    hw = cache_k.shape[3]
    assert tdec & (tdec - 1) == 0 and n_pages % PAGES_PER_STEP == 0
    n_steps = n_pages // PAGES_PER_STEP
    kern = functools.partial(_dec_attn_kernel, n_steps=n_steps, heads=heads, tdec=tdec, lam_init=lam_init)

    def page_spec(pg):
        return pl.BlockSpec((None, None, page, hw),
                            lambda b, s, pt: (pt[b, s * PAGES_PER_STEP + pg], layer, 0, 0))

    grid_spec = pltpu.PrefetchScalarGridSpec(
        num_scalar_prefetch=1,
        grid=(batch, n_steps),
        in_specs=[pl.BlockSpec((4, HEAD_DIM), lambda b, s, pt: (0, 0)),
                  pl.BlockSpec((1, 2 * HEAD_DIM), lambda b, s, pt: (0, 0)),
                  pl.BlockSpec((tdec, hw), lambda b, s, pt: (b, 0)),
                  pl.BlockSpec((page, hw), lambda b, s, pt: (b, 0)),
                  pl.BlockSpec((page, hw), lambda b, s, pt: (b, 0))]
                 + [page_spec(pg) for pg in range(PAGES_PER_STEP)] * 2,
        out_specs=pl.BlockSpec((None, tdec, hw), lambda b, s, pt: (b, 0, 0)),
        scratch_shapes=[pltpu.VMEM((heads, 2 * tdec, 1), F32), pltpu.VMEM((heads, 2 * tdec, 1), F32),
                        pltpu.VMEM((heads, 2 * tdec, 2 * HEAD_DIM), F32)],
    )
    return pl.pallas_call(
        kern,
        out_shape=SDS((batch, tdec, hw), BF16),
        grid_spec=grid_spec,
        compiler_params=_cp(("arbitrary", "arbitrary")),
        name="diff_attn_decode",
    )(page_table, lam_vecs, sub, q_rot, k_new_pad, v_new_pad,
      *([cache_k] * PAGES_PER_STEP), *([cache_v] * PAGES_PER_STEP))


def _causal_conv(xp_ref, x, prev, buf0, is_first, w, tin):
    @pl.when(is_first)
    def _():
        xp_ref[0:SUBLANES, :] = buf0

    @pl.when(jnp.logical_not(is_first))
    def _():
        xp_ref[0:SUBLANES, :] = prev

    xp_ref[SUBLANES:SUBLANES + tin, :] = x
    base = SUBLANES - (CONV_W - 1)
    y = xp_ref[pl.ds(base, tin), :] * w[0:1]
    for j in range(1, CONV_W):
        y = y + xp_ref[pl.ds(base + j, tin), :] * w[j:j + 1]
    return y


def _l2norm(x):
    return x * lax.rsqrt(jnp.sum(x * x, axis=-1, keepdims=True) + EPS)


def _gdn_prep_kernel(x_ref, prev_ref, buf0_ref, cw_ref, gba_ref, avec_ref, dtvec_ref,
                     u_ref, w_ref, qt_ref, kt_ref, qk_ref, eg_ref, xp_ref, *pads, tin, heads):
    c_len = GDN_CHUNK
    dh = HEAD_DIM
    gw = heads * dh
    i = pl.program_id(1)
    y = _silu(_causal_conv(xp_ref, x_ref[...], prev_ref[...], buf0_ref[...], i == 0, cw_ref[...], tin))
    blk = gba_ref[...]
    beta = jax.nn.sigmoid(blk)
    g = -jnp.exp(avec_ref[...]) * jax.nn.softplus(blk + dtvec_ref[...])
    if tin < c_len:
        ypad_ref, gpad_ref, bpad_ref = pads
        ypad_ref[...] = jnp.zeros(ypad_ref.shape, F32)
        gpad_ref[...] = jnp.zeros(gpad_ref.shape, F32)
        bpad_ref[...] = jnp.zeros(bpad_ref.shape, F32)
        ypad_ref[0:tin, :] = y
        gpad_ref[0:tin, :] = g
        bpad_ref[0:tin, :] = beta
        y, g, beta = ypad_ref[...], gpad_ref[...], bpad_ref[...]

    r = lax.broadcasted_iota(jnp.int32, (c_len, c_len), 0)
    c = lax.broadcasted_iota(jnp.int32, (c_len, c_len), 1)
    incl = r >= c
    strict = r > c
    gc = jnp.dot(incl.astype(F32), g, precision=lax.Precision.HIGHEST, preferred_element_type=F32)
    gc_t = gc.T

    for h in range(heads):
        qh = _l2norm(y[:, h * dh:(h + 1) * dh]) * (dh ** -0.5)
        kh = _l2norm(y[:, gw + h * dh:gw + (h + 1) * dh])
        vh = y[:, 2 * gw + h * dh:2 * gw + (h + 1) * dh]
        gcol = gc[:, heads + h:heads + h + 1]
        grow = gc_t[heads + h:heads + h + 1, :]
        bcol = beta[:, h:h + 1]
        glast = gc[c_len - 1:c_len, heads + h:heads + h + 1]
        dmat = jnp.where(incl, jnp.exp(jnp.minimum(gcol - grow, 0.0)), 0.0)
        kb = kh * bcol
        kh16 = kh.astype(BF16)
        lmat = jnp.where(strict, _dot_nt(kb.astype(BF16), kh16) * dmat, 0.0)

        ymat = None
        s = 1
        while s < c_len:
            sh = s.bit_length() - 1
            e = jnp.where(((r >> sh) == (c >> sh) + 1) & ((r >> (sh + 1)) == (c >> (sh + 1))), lmat, 0.0)
            if ymat is None:
                ymat = -e
            else:
                y16 = ymat.astype(BF16)
                f = e + _dot(y16, e.astype(BF16))
                ymat = ymat - (f + _dot(f.astype(BF16), y16))
            s *= 2

        egc = jnp.exp(gcol)
        rhs = jnp.concatenate([vh * bcol, kb * egc], axis=1)
        sol = rhs + _dot(ymat.astype(BF16), rhs.astype(BF16))
        lanes = slice(h * dh, (h + 1) * dh)
        u_ref[:, lanes] = sol[:, :dh]
        w_ref[:, lanes] = sol[:, dh:]
        qt_ref[:, lanes] = qh * egc
        kt_ref[:, lanes] = kh * jnp.exp(glast - gcol)
        qk_ref[:, lanes] = jnp.where(incl, _dot_nt(qh.astype(BF16), kh16) * dmat, 0.0)
        eg_ref[:, lanes] = jnp.broadcast_to(jnp.exp(glast), (SUBLANES, dh))


def _gdn_prep_call(z, row0, batch, seq, buf0, conv_w, avec, dtvec, heads):
    c_len = GDN_CHUNK
    gw = heads * HEAD_DIM
    cw = 3 * gw
    tin = min(seq, c_len)
    n_c = max(seq // c_len, 1)
    assert seq == tin * n_c and row0 % tin == 0
    rows_out = batch * n_c * c_len
    blk0 = row0 // tin
    prev_per_tile = tin // SUBLANES
    prev0 = row0 // SUBLANES
    gba_blk = GBA_COL // LANES
    kern = functools.partial(_gdn_prep_kernel, tin=tin, heads=heads)
    scratch = [pltpu.VMEM((tin + SUBLANES, cw), F32)]
    if tin < c_len:
        scratch += [pltpu.VMEM((c_len, cw), F32), pltpu.VMEM((c_len, LANES), F32), pltpu.VMEM((c_len, LANES), F32)]
    out_blk = pl.BlockSpec((c_len, gw), lambda b, i: (b * n_c + i, 0))
    return pl.pallas_call(
        kern,
        out_shape=tuple(SDS((rows_out, gw), F32) for _ in range(5)) + (SDS((batch * n_c * SUBLANES, gw), F32),),
        grid=(batch, n_c),
        in_specs=[pl.BlockSpec((tin, cw), lambda b, i: (blk0 + b * n_c + i, 0)),
                  pl.BlockSpec((SUBLANES, cw),
                               lambda b, i: (jnp.maximum(prev0 + (b * n_c + i) * prev_per_tile - 1, 0), 0)),
                  pl.BlockSpec((SUBLANES, cw), lambda b, i: (b, 0)),
                  pl.BlockSpec((CONV_W, cw), lambda b, i: (0, 0)),
                  pl.BlockSpec((tin, LANES), lambda b, i: (blk0 + b * n_c + i, gba_blk)),
                  pl.BlockSpec((1, LANES), lambda b, i: (0, 0)),
                  pl.BlockSpec((1, LANES), lambda b, i: (0, 0))],
        out_specs=(out_blk,) * 5 + (pl.BlockSpec((SUBLANES, gw), lambda b, i: (b * n_c + i, 0)),),
        scratch_shapes=scratch,
        compiler_params=_cp(("arbitrary", "arbitrary")),
        name="gdn_prep",
    )(z, z, buf0, conv_w, z, avec, dtvec)


def _gdn_scan_kernel(u_ref, w_ref, qt_ref, kt_ref, qk_ref, eg_ref, s0_ref, gz_ref, gn_ref, o_ref, s_out_ref, s_ref,
                     *, n_c, tin, heads, nb):
    c_len = GDN_CHUNK
    dh = HEAD_DIM
    ci = pl.program_id(1)

    @pl.when(ci == 0)
    def _():
        s_ref[...] = s0_ref[...]

    gn = gn_ref[...]
    for h in range(heads):
        lanes = slice(h * dh, (h + 1) * dh)
        s_h = s_ref[:, h]
        s16 = s_h.astype(BF16)
        wq = jnp.concatenate([w_ref[:, :, lanes], qt_ref[:, :, lanes]], axis=1).astype(BF16)
        ws_qs = jnp.einsum('bck,bkv->bcv', wq, s16, preferred_element_type=F32)
        v_new = u_ref[:, :, lanes] - ws_qs[:, :c_len]
        v16 = v_new.astype(BF16)
        o = ws_qs[:, c_len:] + jnp.einsum('bij,bjv->biv', qk_ref[:, :, lanes].astype(BF16), v16,
                                          preferred_element_type=F32)
        eg = eg_ref[:, 0:1, lanes]
        s_ref[:, h] = s_h * eg + jnp.einsum('bck,bcv->bkv', kt_ref[:, :, lanes].astype(BF16), v16,
                                            preferred_element_type=F32)
        o = o[:, :tin]
        o = o * lax.rsqrt(jnp.mean(o * o, axis=-1, keepdims=True) + EPS) * gn
        o_ref[:, :, lanes] = (o * _silu(gz_ref[:, :, lanes])).astype(o_ref.dtype)

    @pl.when(ci == n_c - 1)
    def _():
        s_out_ref[...] = s_ref[...]


def _gdn_scan_call(prep, s0, z3, gz_blk, gn, batch, seq, heads, nb):
    u, w, qt, kt, qk, eg = prep
    c_len = GDN_CHUNK
    dh = HEAD_DIM
    gw = heads * dh
    tin = min(seq, c_len)
    n_c = max(seq // c_len, 1)
    kern = functools.partial(_gdn_scan_kernel, n_c=n_c, tin=tin, heads=heads, nb=nb)
    blk = pl.BlockSpec((nb, c_len, gw), lambda b, i: (b, i, 0))
    return pl.pallas_call(
        kern,
        out_shape=(SDS((batch, seq, gw), BF16), SDS((batch, heads, dh, dh), F32)),
        grid=(batch // nb, n_c),
        in_specs=[blk, blk, blk, blk, blk,
                  pl.BlockSpec((nb, SUBLANES, gw), lambda b, i: (b, i, 0)),
                  pl.BlockSpec((nb, heads, dh, dh), lambda b, i: (b, 0, 0, 0)),
                  pl.BlockSpec((nb, tin, gw), lambda b, i: (b, i, gz_blk)),
                  pl.BlockSpec((1, dh), lambda b, i: (0, 0))],
        out_specs=(pl.BlockSpec((nb, tin, gw), lambda b, i: (b, i, 0)),
                   pl.BlockSpec((nb, heads, dh, dh), lambda b, i: (b, 0, 0, 0))),
        scratch_shapes=[pltpu.VMEM((nb, heads, dh, dh), F32)],
        compiler_params=_cp(("arbitrary", "arbitrary")),
        name="gdn_scan",
    )(u, w, qt, kt, qk, eg, s0, z3, gn)


def _lru_kernel(x_ref, prev_ref, buf0_ref, cw_ref, cb_ref, wa_ref, ba_ref, wx_ref, bx_ref, lam_ref, h0_ref, lg_ref,
                o_ref, hl_ref, xp_ref, hc_ref, *, tin):
    i = pl.program_id(1)

    @pl.when(i == 0)
    def _():
        hc_ref[...] = h0_ref[...]

    xc = _causal_conv(xp_ref, x_ref[...], prev_ref[...], buf0_ref[...], i == 0, cw_ref[...], tin) + cb_ref[...]
    x16 = xc.astype(BF16)
    rg = jax.nn.sigmoid(_dot(x16, wa_ref[...]) + ba_ref[...])
    ig = jax.nn.sigmoid(_dot(x16, wx_ref[...]) + bx_ref[...])
    log_a = -LRU_C * rg * jax.nn.softplus(-lam_ref[...])
    a = jnp.exp(log_a)
    t = jnp.tanh(log_a)
    b = jnp.sqrt(-2.0 * t / (1.0 - t)) * (ig * xc)

    row = lax.broadcasted_iota(jnp.int32, a.shape, 0)
    d = 1
    while d < tin:
        keep = row >= d
        a_s = jnp.where(keep, pltpu.roll(a, d, 0), 1.0)
        b_s = jnp.where(keep, pltpu.roll(b, d, 0), 0.0)
        b = a * b_s + b
        a = a * a_s
        d *= 2
    hseq = b + a * hc_ref[...]
    hc_ref[...] = hseq[tin - 1:tin, :]
    hl_ref[...] = hseq[tin - 1:tin, :]
    o_ref[...] = (hseq * jax.nn.gelu(lg_ref[...])).astype(o_ref.dtype)


def _lru_call(z, row0, batch, seq, buf0, conv_w, conv_b, wa, ba, wx, bx, lam, h0, col_x, col_g, tin):
    w = conv_w.shape[1]
    n_t = seq // tin
    assert seq == n_t * tin and row0 % tin == 0 and tin & (tin - 1) == 0
    blk0 = row0 // tin
    prev_per_tile = tin // SUBLANES
    prev0 = row0 // SUBLANES
    kern = functools.partial(_lru_kernel, tin=tin)
    vec = pl.BlockSpec((1, w), lambda b, i: (0, 0))
    return pl.pallas_call(
        kern,
        out_shape=(SDS((batch, seq, w), BF16), SDS((batch, 1, w), F32)),
        grid=(batch, n_t),
        in_specs=[pl.BlockSpec((tin, w), lambda b, i: (blk0 + b * n_t + i, col_x // w)),
                  pl.BlockSpec((SUBLANES, w),
                               lambda b, i: (jnp.maximum(prev0 + (b * n_t + i) * prev_per_tile - 1, 0), col_x // w)),
                  pl.BlockSpec((SUBLANES, w), lambda b, i: (b, 0)),
                  pl.BlockSpec((CONV_W, w), lambda b, i: (0, 0)),
                  vec,
                  pl.BlockSpec((w, w), lambda b, i: (0, 0)), vec,
                  pl.BlockSpec((w, w), lambda b, i: (0, 0)), vec,
                  vec,
                  pl.BlockSpec((None, 1, w), lambda b, i: (b, 0, 0)),
                  pl.BlockSpec((tin, w), lambda b, i: (blk0 + b * n_t + i, col_g // w))],
        out_specs=(pl.BlockSpec((None, tin, w), lambda b, i: (b, i, 0)),
                   pl.BlockSpec((None, 1, w), lambda b, i: (b, 0, 0))),
        scratch_shapes=[pltpu.VMEM((tin + SUBLANES, w), F32), pltpu.VMEM((1, w), F32)],
        compiler_params=_cp(("arbitrary", "arbitrary")),
        name="rglru",
    )(z, z, buf0, conv_w, conv_b, wa, ba, wx, bx, lam, h0, z)


GDN_W = 512
DIFF_W = 1024
LRU_W = 512
COL_GQKV = 0
COL_GZ = 3 * GDN_W
COL_DQ = COL_GZ + GDN_W
COL_DK = COL_DQ + DIFF_W
COL_DV = COL_DK + DIFF_W
COL_LX = COL_DV + DIFF_W
COL_LG = COL_LX + LRU_W
GBA_COL = COL_LG + LRU_W
NZ = 6400


def _pad_state_rows(buf):
    b, k, c = buf.shape
    return jnp.pad(buf, ((0, 0), (SUBLANES - k, 0), (0, 0))).reshape(b * SUBLANES, c)


def kernel(x_prompt, x_sample, cache_k, cache_v, state_gdn, state_gdn_conv, state_lru, state_lru_conv, page_table,
           c_prompt, c_sample, ada_w, ada_b, norm_ffn1, ffn1_w_gate, ffn1_w_up, ffn1_w_down, norm_mix, w_in,
           gdn_conv_w, gdn_a_log, gdn_dt_bias, gdn_norm, diff_lq1, diff_lk1, diff_lq2, diff_lk2, diff_subln,
           lru_conv_w, lru_conv_b, lru_wa, lru_ba, lru_wx, lru_bx, lru_lambda, w_out, norm_ffn2, ffn2_w_gate,
           ffn2_w_up, ffn2_w_down, final_norm):
    bp, tp, d = x_prompt.shape
    bd, td, _ = x_sample.shape
    depth = ada_w.shape[0]
    gdn_heads = gdn_a_log.shape[1]
    diff_heads = cache_k.shape[3]
    page = cache_k.shape[2]
    n_pool = cache_k.shape[0]
    past_len = page_table.shape[1] * page
    mp_rows = bp * tp
    m = mp_rows + bd * td
    tm = math.gcd(m, ROW_TILE)
    assert td == SUBLANES and gdn_heads * HEAD_DIM == GDN_W and d == 2048

    x = jnp.concatenate([x_prompt.reshape(mp_rows, d), x_sample.reshape(bd * td, d)], axis=0)
    n_c = bp + bd
    c_all = jnp.pad(jnp.concatenate([c_prompt, c_sample], axis=0), ((0, (-n_c) % SUBLANES), (0, 0)))
    mod = _ada_call(c_all, ada_w, ada_b)
    group_row = jnp.concatenate([jnp.repeat(jnp.arange(bp), tp // SUBLANES),
                                 bp + jnp.repeat(jnp.arange(bd), td // SUBLANES)])

    cache_k4 = cache_k.reshape(n_pool, depth, page, diff_heads * 2 * HEAD_DIM)
    cache_v4 = cache_v.reshape(n_pool, depth, page, diff_heads * 2 * HEAD_DIM)
    zeros_gdn_buf = jnp.zeros((bp * SUBLANES, 3 * GDN_W), F32)
    zeros_lru_buf = jnp.zeros((bp * SUBLANES, LRU_W), F32)

    outs = {k: [] for k in ("k_p", "v_p", "s_p", "sb_p", "h_p", "hb_p", "k_s", "v_s", "s_s", "sb_s", "h_s", "hb_s")}
    for l in range(depth):
        mod8 = mod[l][group_row]
        lam_init = 0.8 - 0.6 * math.exp(-0.3 * l)

        x = _ffn_call(x, norm_ffn1[l][None], mod8, 0, ffn1_w_gate[l].astype(BF16), ffn1_w_up[l].astype(BF16),
                      ffn1_w_down[l].astype(BF16), tm=tm, tf=512)

        wl = w_in[l]
        o1 = 4 * GDN_W
        o2 = o1 + 2 * gdn_heads
        w_re = jnp.concatenate([wl[:, :o1], wl[:, o2:], wl[:, o1:o2]], axis=1)
        w_re = jnp.pad(w_re, ((0, 0), (0, NZ - w_re.shape[1]))).astype(BF16)
        z = _proj_in_call(x, norm_mix[l][None], mod8, 3, w_re, tm=tm, tn=1280)

        lam_vecs = jnp.stack([diff_lq1[l], diff_lk1[l], diff_lq2[l], diff_lk2[l]])
        sub = diff_subln[l][None]
        avec = jnp.zeros((1, LANES), F32).at[0, gdn_heads:2 * gdn_heads].set(gdn_a_log[l])
        dtvec = jnp.zeros((1, LANES), F32).at[0, gdn_heads:2 * gdn_heads].set(gdn_dt_bias[l])
        bw = lru_wa.shape[2]
        eye = jnp.eye(LRU_BLOCKS, dtype=F32)
        wa_full = (eye[:, None, :, None] * lru_wa[l][:, :, None, :]).reshape(LRU_W, LRU_W).astype(BF16)
        wx_full = (eye[:, None, :, None] * lru_wx[l][:, :, None, :]).reshape(LRU_W, LRU_W).astype(BF16)

        mixes = []
        for path in ("p", "s"):
            if path == "p":
                row0, bsz, seq, pos0 = 0, bp, tp, 0
                gdn_buf0, lru_buf0 = zeros_gdn_buf, zeros_lru_buf
                s0 = jnp.zeros((bp, gdn_heads, HEAD_DIM, HEAD_DIM), F32)
                h0 = jnp.zeros((bp, 1, LRU_W), F32)
            else:
                row0, bsz, seq, pos0 = mp_rows, bd, td, past_len
                gdn_buf0 = _pad_state_rows(state_gdn_conv[:, l])
                lru_buf0 = _pad_state_rows(state_lru_conv[:, l])
                s0 = state_gdn[:, l]
                h0 = state_lru[:, l][:, None, :]
            rows = bsz * seq
            z_path = z[row0:row0 + rows]

            prep = _gdn_prep_call(z, row0, bsz, seq, gdn_buf0, gdn_conv_w[l], avec, dtvec, gdn_heads)
            n_chunks = max(seq // GDN_CHUNK, 1)
            prep3 = tuple(p.reshape(bsz, -1, GDN_W) for p in prep)
            gz3 = z[row0:row0 + rows, COL_GZ:COL_GZ + GDN_W].reshape(bsz, seq, GDN_W)
            o_gdn, s_new = _gdn_scan_call(prep3, s0, gz3, 0, gdn_norm[l][None], bsz, seq, gdn_heads,
                                          nb=math.gcd(bsz, 4))

            q_rot, k_rot = _rope_call(z, row0, rows, seq, pos0, COL_DQ, COL_DK, DIFF_W, tq=math.gcd(rows, 256))
            if path == "p":
                o_diff = _flash_call(lam_vecs, sub, q_rot, k_rot, z, COL_DV, bsz, seq, diff_heads, lam_init, tq=256)
                v_rows = z[:rows, COL_DV:COL_DV + DIFF_W]
            else:
                v_rows = z_path[:, COL_DV:COL_DV + DIFF_W]
                pad = ((0, 0), (0, page - seq), (0, 0))
                k_new_pad = jnp.pad(k_rot.reshape(bsz, seq, DIFF_W), pad).reshape(bsz * page, DIFF_W)
                v_new_pad = jnp.pad(v_rows.reshape(bsz, seq, DIFF_W), pad).reshape(bsz * page, DIFF_W)
                o_diff = _dec_attn_call(page_table, lam_vecs, sub, q_rot, k_new_pad, v_new_pad, cache_k4, cache_v4,
                                        l, diff_heads, seq, lam_init)

            o_lru, h_last = _lru_call(z, row0, bsz, seq, lru_buf0, lru_conv_w[l], lru_conv_b[l][None], wa_full,
                                      lru_ba[l][None], wx_full, lru_bx[l][None], lru_lambda[l][None], h0,
                                      COL_LX, COL_LG, tin=min(seq, 256))

            mixes.append(jnp.concatenate([o_gdn.reshape(rows, GDN_W), o_diff.reshape(rows, DIFF_W),
                                          o_lru.reshape(rows, LRU_W)], axis=1))
            zp3 = z_path.reshape(bsz, seq, NZ)
            outs["k_" + path].append(k_rot.reshape(bsz, seq, diff_heads, 2 * HEAD_DIM))
            outs["v_" + path].append(v_rows.reshape(bsz, seq, diff_heads, 2 * HEAD_DIM))
            outs["s_" + path].append(s_new)
            outs["sb_" + path].append(zp3[:, seq - (CONV_W - 1):, COL_GQKV:COL_GQKV + 3 * GDN_W])
            outs["h_" + path].append(h_last.reshape(bsz, LRU_W))
            outs["hb_" + path].append(zp3[:, seq - (CONV_W - 1):, COL_LX:COL_LX + LRU_W])

        mix = jnp.concatenate(mixes, axis=0)
        x = _proj_out_call(mix, w_out[l].astype(BF16), x, mod8, 5, tm=tm, tn=1024)
        x = _ffn_call(x, norm_ffn2[l][None], mod8, 6, ffn2_w_gate[l].astype(BF16), ffn2_w_up[l].astype(BF16),
                      ffn2_w_down[l].astype(BF16), tm=tm, tf=512)

    y_prompt = _final_norm_call(x, final_norm[None], 0, mp_rows, math.gcd(mp_rows, 256)).reshape(bp, tp, d)
    y_sample = _final_norm_call(x, final_norm[None], mp_rows, bd * td, math.gcd(bd * td, 256)).reshape(bd, td, d)
    st = {k: jnp.stack(v, axis=1) for k, v in outs.items()}
    return (y_prompt, y_sample, st["k_p"], st["v_p"], st["s_p"], st["sb_p"], st["h_p"], st["hb_p"],
            st["k_s"], st["v_s"], st["s_s"], st["sb_s"], st["h_s"], st["hb_s"])
```

```python
import functools
import math

import jax
import jax.numpy as jnp
from jax import lax
from jax.experimental import pallas as pl
from jax.experimental.pallas import tpu as pltpu

F32 = jnp.float32
BF16 = jnp.bfloat16
SDS = jax.ShapeDtypeStruct

EPS = 1e-6
HEAD_DIM = 128
CONV_W = 4
ROPE_THETA = 10000.0
LRU_C = 8.0
N_MOD = 9
LRU_BLOCKS = 8
SUBLANES = 8
LANES = 128
VMEM_LIMIT = 60 * 1024 * 1024
GDN_CHUNK = 128
PAGES_PER_STEP = 8
ROW_TILE = 768


def _cp(sem):
    return pltpu.CompilerParams(dimension_semantics=sem, vmem_limit_bytes=VMEM_LIMIT)


def _dot(a, b):
    return jnp.dot(a, b, preferred_element_type=F32)


def _dot_nt(a, b):
    return lax.dot_general(a, b, (((1,), (1,)), ((), ())), preferred_element_type=F32)


def _silu(x):
    return x * jax.nn.sigmoid(x)


def _expand8(v8, tm):
    return v8[:, None, :]


def _norm_mod(x, gain, shift8, scale8):
    tm, d = x.shape
    y = x * lax.rsqrt(jnp.mean(x * x, axis=-1, keepdims=True) + EPS) * gain
    y3 = y.reshape(tm // SUBLANES, SUBLANES, d)
    y3 = y3 * (1.0 + scale8[:, None, :]) + shift8[:, None, :]
    return y3.reshape(tm, d)


def _gated_residual(x, y, gate8, half):
    tm, d = x.shape
    g = gate8 * 0.5 if half else gate8
    y3 = y.reshape(tm // SUBLANES, SUBLANES, d) * g[:, None, :]
    return x + y3.reshape(tm, d)


def _ada_kernel(c_ref, w_ref, b_ref, o_ref):
    c = c_ref[...]
    o_ref[0] = _dot(_silu(c).astype(BF16), w_ref[0].astype(BF16)) + b_ref[0]


def _ada_call(c_all, ada_w, ada_b):
    n_layers, d, n = ada_w.shape
    mp = c_all.shape[0]
    tn = 1024
    return pl.pallas_call(
        _ada_kernel,
        out_shape=SDS((n_layers, mp, n), F32),
        grid=(n_layers, n // tn),
        in_specs=[pl.BlockSpec((mp, d), lambda l, j: (0, 0)),
                  pl.BlockSpec((1, d, tn), lambda l, j: (l, 0, j)),
                  pl.BlockSpec((1, 1, tn), lambda l, j: (l, 0, j))],
        out_specs=pl.BlockSpec((1, mp, tn), lambda l, j: (l, 0, j)),
        compiler_params=_cp(("arbitrary", "arbitrary")),
        name="ada_proj",
    )(c_all, ada_w, ada_b.reshape(n_layers, 1, n))


def _ffn_kernel(x_ref, gain_ref, sh_ref, sc_ref, gt_ref, wg_ref, wu_ref, wd_ref, o_ref, h_ref, *, n_j, col_chunk):
    j = pl.program_id(1)

    @pl.when(j == 0)
    def _():
        h_ref[...] = _norm_mod(x_ref[...], gain_ref[...], sh_ref[...], sc_ref[...]).astype(BF16)
        o_ref[...] = jnp.zeros_like(o_ref)

    h = h_ref[...]
    g = _dot(h, wg_ref[...])
    u = _dot(h, wu_ref[...])
    a = (_silu(g) * u).astype(BF16)
    d = o_ref.shape[1]
    for c in range(0, d, col_chunk):
        o_ref[:, c:c + col_chunk] += _dot(a, wd_ref[:, c:c + col_chunk])

    @pl.when(j == n_j - 1)
    def _():
        o_ref[...] = _gated_residual(x_ref[...], o_ref[...], gt_ref[...], half=True)


def _ffn_call(x, gain, mod8, k_shift, wg, wu, wd, *, tm, tf):
    m, d = x.shape
    f = wg.shape[1]
    n_j = f // tf
    t8 = tm // SUBLANES
    kern = functools.partial(_ffn_kernel, n_j=n_j, col_chunk=512)
    return pl.pallas_call(
        kern,
        out_shape=SDS((m, d), F32),
        grid=(m // tm, n_j),
        in_specs=[pl.BlockSpec((tm, d), lambda i, j: (i, 0)),
                  pl.BlockSpec((1, d), lambda i, j: (0, 0)),
                  pl.BlockSpec((t8, d), lambda i, j: (i, k_shift)),
                  pl.BlockSpec((t8, d), lambda i, j: (i, k_shift + 1)),
                  pl.BlockSpec((t8, d), lambda i, j: (i, k_shift + 2)),
                  pl.BlockSpec((d, tf), lambda i, j: (0, j)),
                  pl.BlockSpec((d, tf), lambda i, j: (0, j)),
                  pl.BlockSpec((tf, d), lambda i, j: (j, 0))],
        out_specs=pl.BlockSpec((tm, d), lambda i, j: (i, 0)),
        scratch_shapes=[pltpu.VMEM((tm, d), BF16)],
        compiler_params=_cp(("arbitrary", "arbitrary")),
        name="ffn_swiglu",
    )(x, gain, mod8, mod8, mod8, wg, wu, wd)


def _proj_in_kernel(x_ref, gain_ref, sh_ref, sc_ref, w_ref, o_ref, h_ref):
    @pl.when(pl.program_id(1) == 0)
    def _():
        h_ref[...] = _norm_mod(x_ref[...], gain_ref[...], sh_ref[...], sc_ref[...]).astype(BF16)

    o_ref[...] = _dot(h_ref[...], w_ref[...])


def _proj_in_call(x, gain, mod8, k_shift, w, *, tm, tn):
    m, d = x.shape
    n = w.shape[1]
    t8 = tm // SUBLANES
    return pl.pallas_call(
        _proj_in_kernel,
        out_shape=SDS((m, n), F32),
        grid=(m // tm, n // tn),
        in_specs=[pl.BlockSpec((tm, d), lambda i, j: (i, 0)),
                  pl.BlockSpec((1, d), lambda i, j: (0, 0)),
                  pl.BlockSpec((t8, d), lambda i, j: (i, k_shift)),
                  pl.BlockSpec((t8, d), lambda i, j: (i, k_shift + 1)),
                  pl.BlockSpec((d, tn), lambda i, j: (0, j))],
        out_specs=pl.BlockSpec((tm, tn), lambda i, j: (i, j)),
        scratch_shapes=[pltpu.VMEM((tm, d), BF16)],
        compiler_params=_cp(("arbitrary", "arbitrary")),
        name="proj_in",
    )(x, gain, mod8, mod8, w)


def _proj_out_kernel(a_ref, w_ref, x_ref, gt_ref, o_ref):
    o_ref[...] = _gated_residual(x_ref[...], _dot(a_ref[...], w_ref[...]), gt_ref[...], half=False)


def _proj_out_call(a, w, x, mod8, k_gate, *, tm, tn):
    m, k = a.shape
    n = w.shape[1]
    t8 = tm // SUBLANES
    gate_blk = (k_gate * x.shape[1]) // tn
    return pl.pallas_call(
        _proj_out_kernel,
        out_shape=SDS((m, n), F32),
        grid=(m // tm, n // tn),
        in_specs=[pl.BlockSpec((tm, k), lambda i, j: (i, 0)),
                  pl.BlockSpec((k, tn), lambda i, j: (0, j)),
                  pl.BlockSpec((tm, tn), lambda i, j: (i, j)),
                  pl.BlockSpec((t8, tn), lambda i, j: (i, gate_blk + j))],
        out_specs=pl.BlockSpec((tm, tn), lambda i, j: (i, j)),
        compiler_params=_cp(("arbitrary", "arbitrary")),
        name="proj_out",
    )(a, w, x, mod8)


def _final_norm_kernel(x_ref, g_ref, o_ref):
    x = x_ref[...]
    o_ref[...] = x * lax.rsqrt(jnp.mean(x * x, axis=-1, keepdims=True) + EPS) * g_ref[...]


def _final_norm_call(x, gain, row0, rows, tm):
    d = x.shape[1]
    blk0 = row0 // tm
    return pl.pallas_call(
        _final_norm_kernel,
        out_shape=SDS((rows, d), F32),
        grid=(rows // tm,),
        in_specs=[pl.BlockSpec((tm, d), lambda i: (blk0 + i, 0)),
                  pl.BlockSpec((1, d), lambda i: (0, 0))],
        out_specs=pl.BlockSpec((tm, d), lambda i: (i, 0)),
        compiler_params=_cp(("arbitrary",)),
        name="final_norm",
    )(x, gain)


def _rope_kernel(q_ref, k_ref, qo_ref, ko_ref, *, tq, seq, pos0):
    i = pl.program_id(0)
    half = HEAD_DIM // 2
    row = i * tq + lax.broadcasted_iota(jnp.int32, (tq, HEAD_DIM), 0)
    lane = lax.broadcasted_iota(jnp.int32, (tq, HEAD_DIM), 1)
    pos = (pos0 + (row & (seq - 1))).astype(F32)
    freq = (lane & (half - 1)).astype(F32)
    inv = jnp.float32(ROPE_THETA) ** (-freq / half)
    ang = pos * inv
    cos = jnp.cos(ang)
    sin = jnp.where(lane < half, -jnp.sin(ang), jnp.sin(ang))
    for src, dst in ((q_ref, qo_ref), (k_ref, ko_ref)):
        for g in range(src.shape[1] // HEAD_DIM):
            x = src[:, g * HEAD_DIM:(g + 1) * HEAD_DIM]
            dst[:, g * HEAD_DIM:(g + 1) * HEAD_DIM] = x * cos + pltpu.roll(x, half, 1) * sin


def _rope_call(z, row0, rows, seq, pos0, col_q, col_k, width, tq):
    assert seq & (seq - 1) == 0 and rows % tq == 0 and row0 % tq == 0
    blk0 = row0 // tq
    kern = functools.partial(_rope_kernel, tq=tq, seq=seq, pos0=pos0)
    return pl.pallas_call(
        kern,
        out_shape=(SDS((rows, width), F32), SDS((rows, width), F32)),
        grid=(rows // tq,),
        in_specs=[pl.BlockSpec((tq, width), lambda i: (blk0 + i, col_q // width)),
                  pl.BlockSpec((tq, width), lambda i: (blk0 + i, col_k // width))],
        out_specs=(pl.BlockSpec((tq, width), lambda i: (i, 0)),
                   pl.BlockSpec((tq, width), lambda i: (i, 0))),
        compiler_params=_cp(("arbitrary",)),
        name="rope_qk",
    )(z, z)


def _diff_lambda(lam_ref, lam_init):
    v = lam_ref[...]
    s1 = jnp.sum(v[0:1] * v[1:2], axis=-1, keepdims=True)
    s2 = jnp.sum(v[2:3] * v[3:4], axis=-1, keepdims=True)
    return jnp.exp(s1) - jnp.exp(s2) + lam_init


def _diff_finish(o1, o2, lam, sub, lam_init):
    o = o1 - lam * o2
    o = o * lax.rsqrt(jnp.mean(o * o, axis=-1, keepdims=True) + EPS) * sub
    return o * (1.0 - lam_init)


def _flash_kernel(lam_ref, sub_ref, q_ref, k_ref, v_ref, o_ref, kb_ref, vb_ref, m_ref, l_ref, acc_ref,
                  *, tq, lam_init):
    i = pl.program_id(2)
    dh = HEAD_DIM
    scale = dh ** -0.5

    @pl.when(i == 0)
    def _():
        kb_ref[...] = k_ref[...].astype(BF16)
        vb_ref[...] = v_ref[...].astype(BF16)

    q = q_ref[...] * scale
    qs = (q[:, :dh].astype(BF16), q[:, dh:].astype(BF16))
    m_ref[...] = jnp.full(m_ref.shape, -jnp.inf, F32)
    l_ref[...] = jnp.zeros(l_ref.shape, F32)
    acc_ref[...] = jnp.zeros(acc_ref.shape, F32)

    def block(j, masked):
        start = pl.multiple_of(j * tq, tq)
        kblk = kb_ref[pl.ds(start, tq), :]
        vblk = vb_ref[pl.ds(start, tq), :]
        for c in range(2):
            s = _dot_nt(qs[c], kblk[:, c * dh:(c + 1) * dh])
            if masked:
                r = lax.broadcasted_iota(jnp.int32, s.shape, 0)
                cc = lax.broadcasted_iota(jnp.int32, s.shape, 1)
                s = jnp.where(cc <= r, s, -jnp.inf)
            m_prev = m_ref[c]
            m_new = jnp.maximum(m_prev, jnp.max(s, axis=-1, keepdims=True))
            alpha = jnp.exp(m_prev - m_new)
            p = jnp.exp(s - m_new)
            l_ref[c] = alpha * l_ref[c] + jnp.sum(p, axis=-1, keepdims=True)
            acc_ref[c] = alpha * acc_ref[c] + _dot(p.astype(BF16), vblk)
            m_ref[c] = m_new

    def body(j, carry):
        block(j, False)
        return carry

    lax.fori_loop(0, i, body, 0)
    block(i, True)

    lam = _diff_lambda(lam_ref, lam_init)
    o1 = acc_ref[0] / l_ref[0]
    o2 = acc_ref[1] / l_ref[1]
    o_ref[...] = _diff_finish(o1, o2, lam, sub_ref[...], lam_init).astype(o_ref.dtype)


def _flash_call(lam_vecs, sub, q_rot, k_rot, z, col_v, batch, seq, heads, lam_init, tq):
    w = 2 * HEAD_DIM
    nq = seq // tq
    kern = functools.partial(_flash_kernel, tq=tq, lam_init=lam_init)
    return pl.pallas_call(
        kern,
        out_shape=SDS((batch * seq, heads * w), BF16),
        grid=(batch, heads, nq),
        in_specs=[pl.BlockSpec((4, HEAD_DIM), lambda b, h, i: (0, 0)),
                  pl.BlockSpec((1, w), lambda b, h, i: (0, 0)),
                  pl.BlockSpec((tq, w), lambda b, h, i: (b * nq + i, h)),
                  pl.BlockSpec((seq, w), lambda b, h, i: (b, h)),
                  pl.BlockSpec((seq, w), lambda b, h, i: (b, col_v // w + h))],
        out_specs=pl.BlockSpec((tq, w), lambda b, h, i: (b * nq + i, h)),
        scratch_shapes=[pltpu.VMEM((seq, w), BF16), pltpu.VMEM((seq, w), BF16),
                        pltpu.VMEM((2, tq, 1), F32), pltpu.VMEM((2, tq, 1), F32), pltpu.VMEM((2, tq, w), F32)],
        compiler_params=_cp(("arbitrary", "arbitrary", "arbitrary")),
        name="diff_flash_prompt",
    )(lam_vecs, sub, q_rot, k_rot, z)


def _dec_attn_kernel(pt_ref, lam_ref, sub_ref, q_ref, kn_ref, vn_ref, *rest, n_steps, heads, tdec, lam_init):
    npg = PAGES_PER_STEP
    k_pages = rest[:npg]
    v_pages = rest[npg:2 * npg]
    o_ref, kb_ref, vb_ref, m_ref, l_ref, acc_ref = rest[2 * npg:]
    s_id = pl.program_id(1)
    dh = HEAD_DIM
    w = 2 * dh
    scale = dh ** -0.5
    page = k_pages[0].shape[0]

    @pl.when(s_id == 0)
    def _():
        m_ref[...] = jnp.full(m_ref.shape, -jnp.inf, F32)
        l_ref[...] = jnp.zeros(l_ref.shape, F32)
        acc_ref[...] = jnp.zeros(acc_ref.shape, F32)

    for pg in range(npg):
        for h in range(heads):
            kb_ref[pg * page:(pg + 1) * page, h * w:(h + 1) * w] = k_pages[pg][:, h, :].astype(BF16)
            vb_ref[pg * page:(pg + 1) * page, h * w:(h + 1) * w] = v_pages[pg][:, h, :].astype(BF16)

    q = q_ref[...]
    lane = lax.broadcasted_iota(jnp.int32, (tdec, w), 1)

    def q_pair(h):
        qh = q[:, h * w:(h + 1) * w]
        return jnp.concatenate([jnp.where(lane < dh, qh, 0.0), jnp.where(lane >= dh, qh, 0.0)], axis=0).astype(BF16)

    def update(h, qp, kblk, vblk, mask):
        s = _dot_nt(qp, kblk) * scale
        if mask is not None:
            s = jnp.where(mask, s, -jnp.inf)
        m_prev = m_ref[h]
        m_new = jnp.maximum(m_prev, jnp.max(s, axis=-1, keepdims=True))
        alpha = jnp.exp(m_prev - m_new)
        p = jnp.exp(s - m_new)
        l_ref[h] = alpha * l_ref[h] + jnp.sum(p, axis=-1, keepdims=True)
        acc_ref[h] = alpha * acc_ref[h] + _dot(p.astype(BF16), vblk)
        m_ref[h] = m_new

    qps = [q_pair(h) for h in range(heads)]
    for h in range(heads):
        update(h, qps[h], kb_ref[:, h * w:(h + 1) * w], vb_ref[:, h * w:(h + 1) * w], None)

    @pl.when(s_id == n_steps - 1)
    def _():
        nk = kn_ref.shape[0]
        r = lax.broadcasted_iota(jnp.int32, (2 * tdec, nk), 0)
        cc = lax.broadcasted_iota(jnp.int32, (2 * tdec, nk), 1)
        mask = cc <= (r & (tdec - 1))
        lam = _diff_lambda(lam_ref, lam_init)
        for h in range(heads):
            update(h, qps[h], kn_ref[:, h * w:(h + 1) * w].astype(BF16), vn_ref[:, h * w:(h + 1) * w].astype(BF16), mask)
            o = acc_ref[h] / l_ref[h]
            o_ref[:, h * w:(h + 1) * w] = _diff_finish(o[:tdec], o[tdec:], lam, sub_ref[...], lam_init).astype(o_ref.dtype)


def _dec_attn_call(page_table, lam_vecs, sub, q_rot, k_new_pad, v_new_pad, cache_k, cache_v, layer, tdec, lam_init):
    batch, n_pages = page_table.shape
    _, _, page, heads, w = cache_k.shape
    hw = heads * w
    assert tdec & (tdec - 1) == 0 and n_pages % PAGES_PER_STEP == 0
    n_steps = n_pages // PAGES_PER_STEP
    kern = functools.partial(_dec_attn_kernel, n_steps=n_steps, heads=heads, tdec=tdec, lam_init=lam_init)

    def page_spec(pg):
        return pl.BlockSpec((None, None, page, heads, w),
                            lambda b, s, pt: (pt[b, s * PAGES_PER_STEP + pg], layer, 0, 0, 0))

    grid_spec = pltpu.PrefetchScalarGridSpec(
        num_scalar_prefetch=1,
        grid=(batch, n_steps),
        in_specs=[pl.BlockSpec((4, HEAD_DIM), lambda b, s, pt: (0, 0)),
                  pl.BlockSpec((1, 2 * HEAD_DIM), lambda b, s, pt: (0, 0)),
                  pl.BlockSpec((tdec, hw), lambda b, s, pt: (b, 0)),
                  pl.BlockSpec((page, hw), lambda b, s, pt: (b, 0)),
                  pl.BlockSpec((page, hw), lambda b, s, pt: (b, 0))]
                 + [page_spec(pg) for pg in range(PAGES_PER_STEP)] * 2,
        out_specs=pl.BlockSpec((None, tdec, hw), lambda b, s, pt: (b, 0, 0)),
        scratch_shapes=[pltpu.VMEM((PAGES_PER_STEP * page, hw), BF16), pltpu.VMEM((PAGES_PER_STEP * page, hw), BF16),
                        pltpu.VMEM((heads, 2 * tdec, 1), F32), pltpu.VMEM((heads, 2 * tdec, 1), F32),
                        pltpu.VMEM((heads, 2 * tdec, 2 * HEAD_DIM), F32)],
    )
    return pl.pallas_call(
        kern,
        out_shape=SDS((batch, tdec, hw), BF16),
        grid_spec=grid_spec,
        compiler_params=_cp(("arbitrary", "arbitrary")),
        name="diff_attn_decode",
    )(page_table, lam_vecs, sub, q_rot, k_new_pad, v_new_pad,
      *([cache_k] * PAGES_PER_STEP), *([cache_v] * PAGES_PER_STEP))


def _causal_conv(xp_ref, x, prev, buf0, is_first, w, tin):
    @pl.when(is_first)
    def _():
        xp_ref[0:SUBLANES, :] = buf0

    @pl.when(jnp.logical_not(is_first))
    def _():
        xp_ref[0:SUBLANES, :] = prev

    xp_ref[SUBLANES:SUBLANES + tin, :] = x
    base = SUBLANES - (CONV_W - 1)
    y = xp_ref[pl.ds(base, tin), :] * w[0:1]
    for j in range(1, CONV_W):
        y = y + xp_ref[pl.ds(base + j, tin), :] * w[j:j + 1]
    return y


def _l2norm(x):
    return x * lax.rsqrt(jnp.sum(x * x, axis=-1, keepdims=True) + EPS)


def _bmm(a, b):
    return jnp.einsum('bij,bjk->bik', a, b, preferred_element_type=F32)


def _bmm_nt(a, b):
    return jnp.einsum('bik,bjk->bij', a, b, preferred_element_type=F32)


def _gdn_chunk_math(y, g, beta, u_ref, w_ref, qt_ref, kt_ref, qk_ref, eg_ref, *, heads, n_levels):
    nb, c_len, _ = y.shape
    dh = HEAD_DIM
    gw = heads * dh
    r = lax.broadcasted_iota(jnp.int32, (c_len, c_len), 0)
    c = lax.broadcasted_iota(jnp.int32, (c_len, c_len), 1)
    incl = r >= c
    strict = r > c
    tri = incl.astype(F32)
    gcs = [jnp.dot(tri, g[b], precision=lax.Precision.HIGHEST, preferred_element_type=F32) for b in range(nb)]
    gc = jnp.stack(gcs)
    gc_t = jnp.stack([x.T for x in gcs])

    for h in range(heads):
        qh = _l2norm(y[:, :, h * dh:(h + 1) * dh]) * (dh ** -0.5)
        kh = _l2norm(y[:, :, gw + h * dh:gw + (h + 1) * dh])
        vh = y[:, :, 2 * gw + h * dh:2 * gw + (h + 1) * dh]
        gcol = gc[:, :, heads + h:heads + h + 1]
        grow = gc_t[:, heads + h:heads + h + 1, :]
        bcol = beta[:, :, h:h + 1]
        glast = gc[:, c_len - 1:c_len, heads + h:heads + h + 1]
        dmat = jnp.where(incl, jnp.exp(jnp.minimum(gcol - grow, 0.0)), 0.0)
        kb = kh * bcol
        kh16 = kh.astype(BF16)
        lmat = jnp.where(strict, _bmm_nt(kb.astype(BF16), kh16) * dmat, 0.0)

        ymat = None
        for sh in range(n_levels):
            e = jnp.where(((r >> sh) == (c >> sh) + 1) & ((r >> (sh + 1)) == (c >> (sh + 1))), lmat, 0.0)
            if ymat is None:
                ymat = -e
            else:
                y16 = ymat.astype(BF16)
                f = e + _bmm(y16, e.astype(BF16))
                ymat = ymat - (f + _bmm(f.astype(BF16), y16))

        egc = jnp.exp(gcol)
        rhs = jnp.concatenate([vh * bcol, kb * egc], axis=2)
        sol = rhs + _bmm(ymat.astype(BF16), rhs.astype(BF16))
        lanes = slice(h * dh, (h + 1) * dh)
        u_ref[:, :, lanes] = sol[:, :, :dh]
        w_ref[:, :, lanes] = sol[:, :, dh:]
        qt_ref[:, :, lanes] = qh * egc
        kt_ref[:, :, lanes] = kh * jnp.exp(glast - gcol)
        qk_ref[:, :, lanes] = jnp.where(incl, _bmm_nt(qh.astype(BF16), kh16) * dmat, 0.0)
        eg_ref[:, :, lanes] = jnp.broadcast_to(jnp.exp(glast), (nb, SUBLANES, dh))


def _gdn_gates(blk, avec, dtvec):
    beta = jax.nn.sigmoid(blk)
    g = -jnp.exp(avec) * jax.nn.softplus(blk + dtvec)
    return g, beta


def _gdn_prep_seq_kernel(x_ref, prev_ref, buf0_ref, cw_ref, gba_ref, avec_ref, dtvec_ref,
                         u_ref, w_ref, qt_ref, kt_ref, qk_ref, eg_ref, xp_ref, *, nb, tiles_per_seq, heads):
    c_len = GDN_CHUNK
    tin = nb * c_len
    is_first = (pl.program_id(0) % tiles_per_seq) == 0
    y = _silu(_causal_conv(xp_ref, x_ref[...], prev_ref[...], buf0_ref[...], is_first, cw_ref[...], tin))
    g, beta = _gdn_gates(gba_ref[...], avec_ref[...], dtvec_ref[...])
    _gdn_chunk_math(y.reshape(nb, c_len, y.shape[1]), g.reshape(nb, c_len, LANES), beta.reshape(nb, c_len, LANES),
                    u_ref, w_ref, qt_ref, kt_ref, qk_ref, eg_ref, heads=heads, n_levels=c_len.bit_length() - 1)


def _gdn_prep_short_kernel(x_ref, buf0_ref, cw_ref, gba_ref, avec_ref, dtvec_ref,
                           u_ref, w_ref, qt_ref, kt_ref, qk_ref, eg_ref, xp_ref, ypad_ref, gpad_ref, bpad_ref,
                           *, nb, seq, heads):
    cw = x_ref.shape[1]
    xp_ref[:, 0:SUBLANES, :] = buf0_ref[...].reshape(nb, SUBLANES, cw)
    xp_ref[:, SUBLANES:SUBLANES + seq, :] = x_ref[...].reshape(nb, seq, cw)
    base = SUBLANES - (CONV_W - 1)
    w = cw_ref[...]
    y = xp_ref[:, pl.ds(base, seq), :] * w[0:1]
    for j in range(1, CONV_W):
        y = y + xp_ref[:, pl.ds(base + j, seq), :] * w[j:j + 1]
    y = _silu(y)
    g, beta = _gdn_gates(gba_ref[...], avec_ref[...], dtvec_ref[...])
    ypad_ref[...] = jnp.zeros(ypad_ref.shape, F32)
    gpad_ref[...] = jnp.zeros(gpad_ref.shape, F32)
    bpad_ref[...] = jnp.zeros(bpad_ref.shape, F32)
    ypad_ref[:, 0:seq, :] = y
    gpad_ref[:, 0:seq, :] = g.reshape(nb, seq, LANES)
    bpad_ref[:, 0:seq, :] = beta.reshape(nb, seq, LANES)
    _gdn_chunk_math(ypad_ref[...], gpad_ref[...], bpad_ref[...], u_ref, w_ref, qt_ref, kt_ref, qk_ref, eg_ref,
                    heads=heads, n_levels=seq.bit_length() - 1)


def _gdn_prep_call(z, row0, batch, seq, buf0, conv_w, avec, dtvec, heads, nb):
    c_len = GDN_CHUNK
    gw = heads * HEAD_DIM
    cw = 3 * gw
    gba_blk = GBA_COL // LANES
    vec = pl.BlockSpec((1, LANES), lambda i: (0, 0))
    if seq >= c_len:
        tin = nb * c_len
        assert seq % tin == 0 and row0 % tin == 0
        tiles_per_seq = seq // tin
        n_tiles = batch * tiles_per_seq
        blk0 = row0 // tin
        prev0 = row0 // SUBLANES
        kern = functools.partial(_gdn_prep_seq_kernel, nb=nb, tiles_per_seq=tiles_per_seq, heads=heads)
        in_specs = [pl.BlockSpec((tin, cw), lambda i: (blk0 + i, 0)),
                    pl.BlockSpec((SUBLANES, cw), lambda i: (jnp.maximum(prev0 + i * (tin // SUBLANES) - 1, 0), 0)),
                    pl.BlockSpec((SUBLANES, cw), lambda i: (i // tiles_per_seq, 0)),
                    pl.BlockSpec((CONV_W, cw), lambda i: (0, 0)),
                    pl.BlockSpec((tin, LANES), lambda i: (blk0 + i, gba_blk)), vec, vec]
        scratch = [pltpu.VMEM((tin + SUBLANES, cw), F32)]
        args = (z, z, buf0, conv_w, z, avec, dtvec)
    else:
        assert seq == SUBLANES and batch % nb == 0 and row0 % (nb * seq) == 0
        tin = nb * seq
        n_tiles = batch // nb
        blk0 = row0 // tin
        kern = functools.partial(_gdn_prep_short_kernel, nb=nb, seq=seq, heads=heads)
        in_specs = [pl.BlockSpec((tin, cw), lambda i: (blk0 + i, 0)),
                    pl.BlockSpec((nb * SUBLANES, cw), lambda i: (i, 0)),
                    pl.BlockSpec((CONV_W, cw), lambda i: (0, 0)),
                    pl.BlockSpec((tin, LANES), lambda i: (blk0 + i, gba_blk)), vec, vec]
        scratch = [pltpu.VMEM((nb, SUBLANES + seq, cw), F32), pltpu.VMEM((nb, c_len, cw), F32),
                   pltpu.VMEM((nb, c_len, LANES), F32), pltpu.VMEM((nb, c_len, LANES), F32)]
        args = (z, buf0, conv_w, z, avec, dtvec)
    n_chunks = n_tiles * nb
    out_blk = pl.BlockSpec((nb, c_len, gw), lambda i: (i, 0, 0))
    return pl.pallas_call(
        kern,
        out_shape=tuple(SDS((n_chunks, c_len, gw), F32) for _ in range(5)) + (SDS((n_chunks, SUBLANES, gw), F32),),
        grid=(n_tiles,),
        in_specs=in_specs,
        out_specs=(out_blk,) * 5 + (pl.BlockSpec((nb, SUBLANES, gw), lambda i: (i, 0, 0)),),
        scratch_shapes=scratch,
        compiler_params=_cp(("arbitrary",)),
        name="gdn_prep",
    )(*args)


def _gdn_scan_kernel(u_ref, w_ref, qt_ref, kt_ref, qk_ref, eg_ref, s0_ref, gz_ref, gn_ref, o_ref, s_out_ref, s_ref,
                     *, n_c, tin, heads, nb):
    c_len = GDN_CHUNK
    dh = HEAD_DIM
    ci = pl.program_id(1)

    @pl.when(ci == 0)
    def _():
        s_ref[...] = s0_ref[...]

    gn = gn_ref[...]
    for h in range(heads):
        lanes = slice(h * dh, (h + 1) * dh)
        s_h = s_ref[:, h]
        s16 = s_h.astype(BF16)
        wq = jnp.concatenate([w_ref[:, :, lanes], qt_ref[:, :, lanes]], axis=1).astype(BF16)
        ws_qs = jnp.einsum('bck,bkv->bcv', wq, s16, preferred_element_type=F32)
        v_new = u_ref[:, :, lanes] - ws_qs[:, :c_len]
        v16 = v_new.astype(BF16)
        o = ws_qs[:, c_len:] + jnp.einsum('bij,bjv->biv', qk_ref[:, :, lanes].astype(BF16), v16,
                                          preferred_element_type=F32)
        eg = eg_ref[:, 0:1, lanes]
        s_ref[:, h] = s_h * eg + jnp.einsum('bck,bcv->bkv', kt_ref[:, :, lanes].astype(BF16), v16,
                                            preferred_element_type=F32)
        o = o[:, :tin]
        o = o * lax.rsqrt(jnp.mean(o * o, axis=-1, keepdims=True) + EPS) * gn
        o_ref[:, :, lanes] = (o * _silu(gz_ref[:, :, lanes])).astype(o_ref.dtype)

    @pl.when(ci == n_c - 1)
    def _():
        s_out_ref[...] = s_ref[...]


def _gdn_scan_call(prep, s0, z3, gz_blk, gn, batch, seq, heads, nb):
    u, w, qt, kt, qk, eg = prep
    c_len = GDN_CHUNK
    dh = HEAD_DIM
    gw = heads * dh
    tin = min(seq, c_len)
    n_c = max(seq // c_len, 1)
    kern = functools.partial(_gdn_scan_kernel, n_c=n_c, tin=tin, heads=heads, nb=nb)
    blk = pl.BlockSpec((nb, c_len, gw), lambda b, i: (b, i, 0))
    return pl.pallas_call(
        kern,
        out_shape=(SDS((batch, seq, gw), BF16), SDS((batch, heads, dh, dh), F32)),
        grid=(batch // nb, n_c),
        in_specs=[blk, blk, blk, blk, blk,
                  pl.BlockSpec((nb, SUBLANES, gw), lambda b, i: (b, i, 0)),
                  pl.BlockSpec((nb, heads, dh, dh), lambda b, i: (b, 0, 0, 0)),
                  pl.BlockSpec((nb, tin, gw), lambda b, i: (b, i, gz_blk)),
                  pl.BlockSpec((1, dh), lambda b, i: (0, 0))],
        out_specs=(pl.BlockSpec((nb, tin, gw), lambda b, i: (b, i, 0)),
                   pl.BlockSpec((nb, heads, dh, dh), lambda b, i: (b, 0, 0, 0))),
        scratch_shapes=[pltpu.VMEM((nb, heads, dh, dh), F32)],
        compiler_params=_cp(("arbitrary", "arbitrary")),
        name="gdn_scan",
    )(u, w, qt, kt, qk, eg, s0, z3, gn)


def _lru_kernel(x_ref, prev_ref, buf0_ref, cw_ref, cb_ref, wa_ref, ba_ref, wx_ref, bx_ref, lam_ref, h0_ref, lg_ref,
                o_ref, hl_ref, xp_ref, hc_ref, *, tin):
    i = pl.program_id(1)

    @pl.when(i == 0)
    def _():
        hc_ref[...] = h0_ref[...]

    xc = _causal_conv(xp_ref, x_ref[...], prev_ref[...], buf0_ref[...], i == 0, cw_ref[...], tin) + cb_ref[...]
    x16 = xc.astype(BF16)
    rg = jax.nn.sigmoid(_dot(x16, wa_ref[...]) + ba_ref[...])
    ig = jax.nn.sigmoid(_dot(x16, wx_ref[...]) + bx_ref[...])
    log_a = -LRU_C * rg * jax.nn.softplus(-lam_ref[...])
    a = jnp.exp(log_a)
    t = jnp.tanh(log_a)
    b = jnp.sqrt(-2.0 * t / (1.0 - t)) * (ig * xc)

    row = lax.broadcasted_iota(jnp.int32, a.shape, 0)
    d = 1
    while d < tin:
        keep = row >= d
        a_s = jnp.where(keep, pltpu.roll(a, d, 0), 1.0)
        b_s = jnp.where(keep, pltpu.roll(b, d, 0), 0.0)
        b = a * b_s + b
        a = a * a_s
        d *= 2
    hseq = b + a * hc_ref[...]
    hc_ref[...] = hseq[tin - 1:tin, :]
    hl_ref[...] = hseq[tin - 1:tin, :]
    o_ref[...] = (hseq * jax.nn.gelu(lg_ref[...])).astype(o_ref.dtype)


def _lru_call(z, row0, batch, seq, buf0, conv_w, conv_b, wa, ba, wx, bx, lam, h0, col_x, col_g, tin):
    w = conv_w.shape[1]
    n_t = seq // tin
    assert seq == n_t * tin and row0 % tin == 0 and tin & (tin - 1) == 0
    blk0 = row0 // tin
    prev_per_tile = tin // SUBLANES
    prev0 = row0 // SUBLANES
    kern = functools.partial(_lru_kernel, tin=tin)
    vec = pl.BlockSpec((1, w), lambda b, i: (0, 0))
    return pl.pallas_call(
        kern,
        out_shape=(SDS((batch, seq, w), BF16), SDS((batch, 1, w), F32)),
        grid=(batch, n_t),
        in_specs=[pl.BlockSpec((tin, w), lambda b, i: (blk0 + b * n_t + i, col_x // w)),
                  pl.BlockSpec((SUBLANES, w),
                               lambda b, i: (jnp.maximum(prev0 + (b * n_t + i) * prev_per_tile - 1, 0), col_x // w)),
                  pl.BlockSpec((SUBLANES, w), lambda b, i: (b, 0)),
                  pl.BlockSpec((CONV_W, w), lambda b, i: (0, 0)),
                  vec,
                  pl.BlockSpec((w, w), lambda b, i: (0, 0)), vec,
                  pl.BlockSpec((w, w), lambda b, i: (0, 0)), vec,
                  vec,
                  pl.BlockSpec((None, 1, w), lambda b, i: (b, 0, 0)),
                  pl.BlockSpec((tin, w), lambda b, i: (blk0 + b * n_t + i, col_g // w))],
        out_specs=(pl.BlockSpec((None, tin, w), lambda b, i: (b, i, 0)),
                   pl.BlockSpec((None, 1, w), lambda b, i: (b, 0, 0))),
        scratch_shapes=[pltpu.VMEM((tin + SUBLANES, w), F32), pltpu.VMEM((1, w), F32)],
        compiler_params=_cp(("arbitrary", "arbitrary")),
        name="rglru",
    )(z, z, buf0, conv_w, conv_b, wa, ba, wx, bx, lam, h0, z)


GDN_W = 512
DIFF_W = 1024
LRU_W = 512
COL_GQKV = 0
COL_GZ = 3 * GDN_W
COL_DQ = COL_GZ + GDN_W
COL_DK = COL_DQ + DIFF_W
COL_DV = COL_DK + DIFF_W
COL_LX = COL_DV + DIFF_W
COL_LG = COL_LX + LRU_W
GBA_COL = COL_LG + LRU_W
NZ = 6400


def _pad_state_rows(buf):
    b, k, c = buf.shape
    return jnp.pad(buf, ((0, 0), (SUBLANES - k, 0), (0, 0))).reshape(b * SUBLANES, c)


def kernel(x_prompt, x_sample, cache_k, cache_v, state_gdn, state_gdn_conv, state_lru, state_lru_conv, page_table,
           c_prompt, c_sample, ada_w, ada_b, norm_ffn1, ffn1_w_gate, ffn1_w_up, ffn1_w_down, norm_mix, w_in,
           gdn_conv_w, gdn_a_log, gdn_dt_bias, gdn_norm, diff_lq1, diff_lk1, diff_lq2, diff_lk2, diff_subln,
           lru_conv_w, lru_conv_b, lru_wa, lru_ba, lru_wx, lru_bx, lru_lambda, w_out, norm_ffn2, ffn2_w_gate,
           ffn2_w_up, ffn2_w_down, final_norm):
    bp, tp, d = x_prompt.shape
    bd, td, _ = x_sample.shape
    depth = ada_w.shape[0]
    gdn_heads = gdn_a_log.shape[1]
    diff_heads = cache_k.shape[3]
    page = cache_k.shape[2]
    past_len = page_table.shape[1] * page
    mp_rows = bp * tp
    m = mp_rows + bd * td
    tm = math.gcd(m, ROW_TILE)
    assert td == SUBLANES and gdn_heads * HEAD_DIM == GDN_W and d == 2048

    x = jnp.concatenate([x_prompt.reshape(mp_rows, d), x_sample.reshape(bd * td, d)], axis=0)
    n_c = bp + bd
    c_all = jnp.pad(jnp.concatenate([c_prompt, c_sample], axis=0), ((0, (-n_c) % SUBLANES), (0, 0)))
    mod = _ada_call(c_all, ada_w, ada_b)
    group_row = jnp.concatenate([jnp.repeat(jnp.arange(bp), tp // SUBLANES),
                                 bp + jnp.repeat(jnp.arange(bd), td // SUBLANES)])

    zeros_gdn_buf = jnp.zeros((bp * SUBLANES, 3 * GDN_W), F32)
    zeros_lru_buf = jnp.zeros((bp * SUBLANES, LRU_W), F32)

    outs = {k: [] for k in ("k_p", "v_p", "s_p", "sb_p", "h_p", "hb_p", "k_s", "v_s", "s_s", "sb_s", "h_s", "hb_s")}
    for l in range(depth):
        mod8 = mod[l][group_row]
        lam_init = 0.8 - 0.6 * math.exp(-0.3 * l)

        x = _ffn_call(x, norm_ffn1[l][None], mod8, 0, ffn1_w_gate[l].astype(BF16), ffn1_w_up[l].astype(BF16),
                      ffn1_w_down[l].astype(BF16), tm=tm, tf=512)

        wl = w_in[l]
        o1 = 4 * GDN_W
        o2 = o1 + 2 * gdn_heads
        w_re = jnp.concatenate([wl[:, :o1], wl[:, o2:], wl[:, o1:o2]], axis=1)
        w_re = jnp.pad(w_re, ((0, 0), (0, NZ - w_re.shape[1]))).astype(BF16)
        z = _proj_in_call(x, norm_mix[l][None], mod8, 3, w_re, tm=tm, tn=1280)

        lam_vecs = jnp.stack([diff_lq1[l], diff_lk1[l], diff_lq2[l], diff_lk2[l]])
        sub = diff_subln[l][None]
        avec = jnp.zeros((1, LANES), F32).at[0, gdn_heads:2 * gdn_heads].set(gdn_a_log[l])
        dtvec = jnp.zeros((1, LANES), F32).at[0, gdn_heads:2 * gdn_heads].set(gdn_dt_bias[l])
        bw = lru_wa.shape[2]
        eye = jnp.eye(LRU_BLOCKS, dtype=F32)
        wa_full = (eye[:, None, :, None] * lru_wa[l][:, :, None, :]).reshape(LRU_W, LRU_W).astype(BF16)
        wx_full = (eye[:, None, :, None] * lru_wx[l][:, :, None, :]).reshape(LRU_W, LRU_W).astype(BF16)

        mixes = []
        for path in ("p", "s"):
            if path == "p":
                row0, bsz, seq, pos0 = 0, bp, tp, 0
                gdn_buf0, lru_buf0 = zeros_gdn_buf, zeros_lru_buf
                s0 = jnp.zeros((bp, gdn_heads, HEAD_DIM, HEAD_DIM), F32)
                h0 = jnp.zeros((bp, 1, LRU_W), F32)
            else:
                row0, bsz, seq, pos0 = mp_rows, bd, td, past_len
                gdn_buf0 = _pad_state_rows(state_gdn_conv[:, l])
                lru_buf0 = _pad_state_rows(state_lru_conv[:, l])
                s0 = state_gdn[:, l]
                h0 = state_lru[:, l][:, None, :]
            rows = bsz * seq
            z_path = z[row0:row0 + rows]

            prep = _gdn_prep_call(z, row0, bsz, seq, gdn_buf0, gdn_conv_w[l], avec, dtvec, gdn_heads,
                                  nb=math.gcd(4, seq // GDN_CHUNK) if path == "p" else math.gcd(bsz, 8))
            prep3 = tuple(p.reshape(bsz, -1, GDN_W) for p in prep)
            gz3 = z[row0:row0 + rows, COL_GZ:COL_GZ + GDN_W].reshape(bsz, seq, GDN_W)
            o_gdn, s_new = _gdn_scan_call(prep3, s0, gz3, 0, gdn_norm[l][None], bsz, seq, gdn_heads,
                                          nb=math.gcd(bsz, 4))

            q_rot, k_rot = _rope_call(z, row0, rows, seq, pos0, COL_DQ, COL_DK, DIFF_W, tq=math.gcd(rows, 256))
            if path == "p":
                o_diff = _flash_call(lam_vecs, sub, q_rot, k_rot, z, COL_DV, bsz, seq, diff_heads, lam_init,
                                     tq=math.gcd(seq, 512))
                v_rows = z[:rows, COL_DV:COL_DV + DIFF_W]
            else:
                v_rows = z_path[:, COL_DV:COL_DV + DIFF_W]
                pad = ((0, 0), (0, page - seq), (0, 0))
                k_new_pad = jnp.pad(k_rot.reshape(bsz, seq, DIFF_W), pad).reshape(bsz * page, DIFF_W)
                v_new_pad = jnp.pad(v_rows.reshape(bsz, seq, DIFF_W), pad).reshape(bsz * page, DIFF_W)
                o_diff = _dec_attn_call(page_table, lam_vecs, sub, q_rot, k_new_pad, v_new_pad, cache_k, cache_v,
                                        l, seq, lam_init)

            o_lru, h_last = _lru_call(z, row0, bsz, seq, lru_buf0, lru_conv_w[l], lru_conv_b[l][None], wa_full,
                                      lru_ba[l][None], wx_full, lru_bx[l][None], lru_lambda[l][None], h0,
                                      COL_LX, COL_LG, tin=min(seq, 256))

            mixes.append(jnp.concatenate([o_gdn.reshape(rows, GDN_W), o_diff.reshape(rows, DIFF_W),
                                          o_lru.reshape(rows, LRU_W)], axis=1))
            zp3 = z_path.reshape(bsz, seq, NZ)
            outs["k_" + path].append(k_rot.reshape(bsz, seq, diff_heads, 2 * HEAD_DIM))
            outs["v_" + path].append(v_rows.reshape(bsz, seq, diff_heads, 2 * HEAD_DIM))
            outs["s_" + path].append(s_new)
            outs["sb_" + path].append(zp3[:, seq - (CONV_W - 1):, COL_GQKV:COL_GQKV + 3 * GDN_W])
            outs["h_" + path].append(h_last.reshape(bsz, LRU_W))
            outs["hb_" + path].append(zp3[:, seq - (CONV_W - 1):, COL_LX:COL_LX + LRU_W])

        mix = jnp.concatenate(mixes, axis=0)
        x = _proj_out_call(mix, w_out[l].astype(BF16), x, mod8, 5, tm=tm, tn=1024)
        x = _ffn_call(x, norm_ffn2[l][None], mod8, 6, ffn2_w_gate[l].astype(BF16), ffn2_w_up[l].astype(BF16),
                      ffn2_w_down[l].astype(BF16), tm=tm, tf=512)

    y_prompt = _final_norm_call(x, final_norm[None], 0, mp_rows, math.gcd(mp_rows, 256)).reshape(bp, tp, d)
    y_sample = _final_norm_call(x, final_norm[None], mp_rows, bd * td, math.gcd(bd * td, 256)).reshape(bd, td, d)
    st = {k: jnp.stack(v, axis=1) for k, v in outs.items()}
    return (y_prompt, y_sample, st["k_p"], st["v_p"], st["s_p"], st["sb_p"], st["h_p"], st["hb_p"],
            st["k_s"], st["v_s"], st["s_s"], st["sb_s"], st["h_s"], st["hb_s"])
```

```python
import functools
import math
from typing import NamedTuple

import jax
import jax.numpy as jnp
from jax import lax
from jax.experimental import pallas as pl
from jax.experimental.pallas import tpu as pltpu

F32 = jnp.float32
BF16 = jnp.bfloat16
SDS = jax.ShapeDtypeStruct

EPS = 1e-6
HEAD_DIM = 128
CONV_W = 4
ROPE_THETA = 10000.0
LRU_C = 8.0
N_MOD = 9
LRU_BLOCKS = 8
SUBLANES = 8
LANES = 128
VMEM_LIMIT = 60 * 1024 * 1024
GDN_CHUNK = 128
PAGES_PER_STEP = 8
ROW_TILE = 768


def _cp(sem):
    return pltpu.CompilerParams(dimension_semantics=sem, vmem_limit_bytes=VMEM_LIMIT)


def _dot(a, b):
    return jnp.dot(a, b, preferred_element_type=F32)


def _dot_nt(a, b):
    return lax.dot_general(a, b, (((1,), (1,)), ((), ())), preferred_element_type=F32)


def _silu(x):
    return x * jax.nn.sigmoid(x)


class _RowMap(NamedTuple):
    groups_prompt: int
    shift_prompt: int
    n_prompt: int


def _group_mods(mod_ref, tile, t8, rmap):
    mp = mod_ref.shape[0]
    g = tile * t8 + lax.broadcasted_iota(jnp.int32, (t8, mp), 0)
    col = lax.broadcasted_iota(jnp.int32, (t8, mp), 1)
    src = jnp.where(g < rmap.groups_prompt, g >> rmap.shift_prompt, g - rmap.groups_prompt + rmap.n_prompt)
    return jnp.dot((col == src).astype(F32), mod_ref[...], precision=lax.Precision.HIGHEST, preferred_element_type=F32)


def _norm_mod(x, gain, shift8, scale8):
    tm, d = x.shape
    y = x * lax.rsqrt(jnp.mean(x * x, axis=-1, keepdims=True) + EPS) * gain
    y3 = y.reshape(tm // SUBLANES, SUBLANES, d)
    y3 = y3 * (1.0 + scale8[:, None, :]) + shift8[:, None, :]
    return y3.reshape(tm, d)


def _gated_residual(x, y, gate8, half):
    tm, d = x.shape
    g = gate8 * 0.5 if half else gate8
    y3 = y.reshape(tm // SUBLANES, SUBLANES, d) * g[:, None, :]
    return x + y3.reshape(tm, d)


def _ada_kernel(c_ref, w_ref, b_ref, o_ref):
    c = c_ref[...]
    o_ref[0] = _dot(_silu(c).astype(BF16), w_ref[0].astype(BF16)) + b_ref[0]


def _ada_call(c_all, ada_w, ada_b):
    n_layers, d, n = ada_w.shape
    mp = c_all.shape[0]
    tn = 1024
    return pl.pallas_call(
        _ada_kernel,
        out_shape=SDS((n_layers, mp, n), F32),
        grid=(n_layers, n // tn),
        in_specs=[pl.BlockSpec((mp, d), lambda l, j: (0, 0)),
                  pl.BlockSpec((1, d, tn), lambda l, j: (l, 0, j)),
                  pl.BlockSpec((1, 1, tn), lambda l, j: (l, 0, j))],
        out_specs=pl.BlockSpec((1, mp, tn), lambda l, j: (l, 0, j)),
        compiler_params=_cp(("arbitrary", "arbitrary")),
        name="ada_proj",
    )(c_all, ada_w, ada_b.reshape(n_layers, 1, n))


def _ffn_kernel(x_ref, gain_ref, sh_ref, sc_ref, gt_ref, wg_ref, wu_ref, wd_ref, o_ref, h_ref, *, n_j, col_chunk, rmap):
    i = pl.program_id(0)
    j = pl.program_id(1)
    t8 = x_ref.shape[0] // SUBLANES

    @pl.when(j == 0)
    def _():
        sh8 = _group_mods(sh_ref, i, t8, rmap)
        sc8 = _group_mods(sc_ref, i, t8, rmap)
        h_ref[...] = _norm_mod(x_ref[...], gain_ref[...], sh8, sc8).astype(BF16)
        o_ref[...] = jnp.zeros_like(o_ref)

    h = h_ref[...]
    g = _dot(h, wg_ref[...])
    u = _dot(h, wu_ref[...])
    a = (_silu(g) * u).astype(BF16)
    d = o_ref.shape[1]
    for c in range(0, d, col_chunk):
        o_ref[:, c:c + col_chunk] += _dot(a, wd_ref[:, c:c + col_chunk])

    @pl.when(j == n_j - 1)
    def _():
        o_ref[...] = _gated_residual(x_ref[...], o_ref[...], _group_mods(gt_ref, i, t8, rmap), half=True)


def _mod_spec(mod, layer, k):
    mp = mod.shape[1]
    d = mod.shape[2] // N_MOD
    return pl.BlockSpec((None, mp, d), lambda i, j: (layer, 0, k))


def _ffn_call(x, gain, mod, layer, k_shift, wg, wu, wd, rmap, *, tm, tf):
    m, d = x.shape
    f = wg.shape[2]
    n_j = f // tf
    kern = functools.partial(_ffn_kernel, n_j=n_j, col_chunk=512, rmap=rmap)
    return pl.pallas_call(
        kern,
        out_shape=SDS((m, d), F32),
        grid=(m // tm, n_j),
        in_specs=[pl.BlockSpec((tm, d), lambda i, j: (i, 0)),
                  pl.BlockSpec((1, d), lambda i, j: (0, 0)),
                  _mod_spec(mod, layer, k_shift), _mod_spec(mod, layer, k_shift + 1), _mod_spec(mod, layer, k_shift + 2),
                  pl.BlockSpec((None, d, tf), lambda i, j: (layer, 0, j)),
                  pl.BlockSpec((None, d, tf), lambda i, j: (layer, 0, j)),
                  pl.BlockSpec((None, tf, d), lambda i, j: (layer, j, 0))],
        out_specs=pl.BlockSpec((tm, d), lambda i, j: (i, 0)),
        scratch_shapes=[pltpu.VMEM((tm, d), BF16)],
        compiler_params=_cp(("arbitrary", "arbitrary")),
        name="ffn_swiglu",
    )(x, gain, mod, mod, mod, wg, wu, wd)


def _proj_in_kernel(x_ref, gain_ref, sh_ref, sc_ref, w_ref, o_ref, h_ref, *, rmap):
    @pl.when(pl.program_id(1) == 0)
    def _():
        i = pl.program_id(0)
        t8 = x_ref.shape[0] // SUBLANES
        sh8 = _group_mods(sh_ref, i, t8, rmap)
        sc8 = _group_mods(sc_ref, i, t8, rmap)
        h_ref[...] = _norm_mod(x_ref[...], gain_ref[...], sh8, sc8).astype(BF16)

    o_ref[...] = _dot(h_ref[...], w_ref[...])


def _proj_in_call(x, gain, mod, layer, k_shift, w, rmap, *, tm, tn):
    m, d = x.shape
    n = w.shape[1]
    return pl.pallas_call(
        functools.partial(_proj_in_kernel, rmap=rmap),
        out_shape=SDS((m, n), F32),
        grid=(m // tm, n // tn),
        in_specs=[pl.BlockSpec((tm, d), lambda i, j: (i, 0)),
                  pl.BlockSpec((1, d), lambda i, j: (0, 0)),
                  _mod_spec(mod, layer, k_shift), _mod_spec(mod, layer, k_shift + 1),
                  pl.BlockSpec((d, tn), lambda i, j: (0, j))],
        out_specs=pl.BlockSpec((tm, tn), lambda i, j: (i, j)),
        scratch_shapes=[pltpu.VMEM((tm, d), BF16)],
        compiler_params=_cp(("arbitrary", "arbitrary")),
        name="proj_in",
    )(x, gain, mod, mod, w)


def _proj_out_kernel(*refs, n_mix, n_prompt_tiles, rmap):
    prompt = refs[:n_mix]
    decode = refs[n_mix:2 * n_mix]
    w_ref, x_ref, gt_ref, o_ref = refs[2 * n_mix:]
    i = pl.program_id(0)
    t8 = x_ref.shape[0] // SUBLANES

    def run(parts):
        k0 = 0
        y = None
        for a_ref in parts:
            kw = a_ref.shape[1]
            part = _dot(a_ref[...], w_ref[k0:k0 + kw, :])
            y = part if y is None else y + part
            k0 += kw
        o_ref[...] = _gated_residual(x_ref[...], y, _group_mods(gt_ref, i, t8, rmap), half=False)

    @pl.when(i < n_prompt_tiles)
    def _():
        run(prompt)

    @pl.when(i >= n_prompt_tiles)
    def _():
        run(decode)


def _proj_out_call(prompt_parts, decode_parts, w, layer, x, mod, k_gate, rmap, *, tm):
    m, d = x.shape
    rows_p = prompt_parts[0].shape[0]
    assert rows_p % tm == 0 and (m - rows_p) % tm == 0
    npt = rows_p // tm
    n_mix = len(prompt_parts)
    kern = functools.partial(_proj_out_kernel, n_mix=n_mix, n_prompt_tiles=npt, rmap=rmap)
    p_specs = [pl.BlockSpec((tm, a.shape[1]), lambda i, j: (jnp.minimum(i, npt - 1), 0)) for a in prompt_parts]
    s_specs = [pl.BlockSpec((tm, a.shape[1]), lambda i, j: (jnp.maximum(i - npt, 0), 0)) for a in decode_parts]
    return pl.pallas_call(
        kern,
        out_shape=SDS((m, d), F32),
        grid=(m // tm, 1),
        in_specs=p_specs + s_specs + [pl.BlockSpec((None, w.shape[1], d), lambda i, j: (layer, 0, 0)),
                                      pl.BlockSpec((tm, d), lambda i, j: (i, 0)),
                                      _mod_spec(mod, layer, k_gate)],
        out_specs=pl.BlockSpec((tm, d), lambda i, j: (i, 0)),
        compiler_params=_cp(("arbitrary", "arbitrary")),
        name="proj_out",
    )(*prompt_parts, *decode_parts, w, x, mod)


def _final_norm_kernel(x_ref, g_ref, o_ref):
    x = x_ref[...]
    o_ref[...] = x * lax.rsqrt(jnp.mean(x * x, axis=-1, keepdims=True) + EPS) * g_ref[...]


def _final_norm_call(x, gain, row0, rows, tm):
    d = x.shape[1]
    blk0 = row0 // tm
    return pl.pallas_call(
        _final_norm_kernel,
        out_shape=SDS((rows, d), F32),
        grid=(rows // tm,),
        in_specs=[pl.BlockSpec((tm, d), lambda i: (blk0 + i, 0)),
                  pl.BlockSpec((1, d), lambda i: (0, 0))],
        out_specs=pl.BlockSpec((tm, d), lambda i: (i, 0)),
        compiler_params=_cp(("arbitrary",)),
        name="final_norm",
    )(x, gain)


def _rope_kernel(q_ref, k_ref, v_ref, qo_ref, ko_ref, vo_ref, *, tq, seq, pos0):
    i = pl.program_id(0)
    vo_ref[...] = v_ref[...]
    half = HEAD_DIM // 2
    row = i * tq + lax.broadcasted_iota(jnp.int32, (tq, HEAD_DIM), 0)
    lane = lax.broadcasted_iota(jnp.int32, (tq, HEAD_DIM), 1)
    pos = (pos0 + (row & (seq - 1))).astype(F32)
    freq = (lane & (half - 1)).astype(F32)
    inv = jnp.float32(ROPE_THETA) ** (-freq / half)
    ang = pos * inv
    cos = jnp.cos(ang)
    sin = jnp.where(lane < half, -jnp.sin(ang), jnp.sin(ang))
    for src, dst in ((q_ref, qo_ref), (k_ref, ko_ref)):
        for g in range(src.shape[1] // HEAD_DIM):
            x = src[:, g * HEAD_DIM:(g + 1) * HEAD_DIM]
            dst[:, g * HEAD_DIM:(g + 1) * HEAD_DIM] = x * cos + pltpu.roll(x, half, 1) * sin


def _rope_call(z, row0, rows, seq, pos0, col_q, col_k, col_v, width, tq):
    assert seq & (seq - 1) == 0 and rows % tq == 0 and row0 % tq == 0
    blk0 = row0 // tq
    kern = functools.partial(_rope_kernel, tq=tq, seq=seq, pos0=pos0)
    out_blk = pl.BlockSpec((tq, width), lambda i: (i, 0))
    return pl.pallas_call(
        kern,
        out_shape=(SDS((rows, width), F32),) * 3,
        grid=(rows // tq,),
        in_specs=[pl.BlockSpec((tq, width), lambda i: (blk0 + i, col_q // width)),
                  pl.BlockSpec((tq, width), lambda i: (blk0 + i, col_k // width)),
                  pl.BlockSpec((tq, width), lambda i: (blk0 + i, col_v // width))],
        out_specs=(out_blk,) * 3,
        compiler_params=_cp(("arbitrary",)),
        name="rope_qk",
    )(z, z, z)


def _diff_lambda(lam_ref, lam_init):
    v = lam_ref[...]
    s1 = jnp.sum(v[0:1] * v[1:2], axis=-1, keepdims=True)
    s2 = jnp.sum(v[2:3] * v[3:4], axis=-1, keepdims=True)
    return jnp.exp(s1) - jnp.exp(s2) + lam_init


def _diff_finish(o1, o2, lam, sub, lam_init):
    o = o1 - lam * o2
    o = o * lax.rsqrt(jnp.mean(o * o, axis=-1, keepdims=True) + EPS) * sub
    return o * (1.0 - lam_init)


def _flash_kernel(lam_ref, sub_ref, q_ref, k_ref, v_ref, o_ref, kb_ref, vb_ref, m_ref, l_ref, acc_ref,
                  *, tq, lam_init):
    i = pl.program_id(2)
    dh = HEAD_DIM
    scale = dh ** -0.5

    @pl.when(i == 0)
    def _():
        kb_ref[...] = k_ref[...].astype(BF16)
        vb_ref[...] = v_ref[...].astype(BF16)

    q = q_ref[...] * scale
    qs = (q[:, :dh].astype(BF16), q[:, dh:].astype(BF16))
    m_ref[...] = jnp.full(m_ref.shape, -jnp.inf, F32)
    l_ref[...] = jnp.zeros(l_ref.shape, F32)
    acc_ref[...] = jnp.zeros(acc_ref.shape, F32)

    def block(j, masked):
        start = pl.multiple_of(j * tq, tq)
        kblk = kb_ref[pl.ds(start, tq), :]
        vblk = vb_ref[pl.ds(start, tq), :]
        for c in range(2):
            s = _dot_nt(qs[c], kblk[:, c * dh:(c + 1) * dh])
            if masked:
                r = lax.broadcasted_iota(jnp.int32, s.shape, 0)
                cc = lax.broadcasted_iota(jnp.int32, s.shape, 1)
                s = jnp.where(cc <= r, s, -jnp.inf)
            m_prev = m_ref[c]
            m_new = jnp.maximum(m_prev, jnp.max(s, axis=-1, keepdims=True))
            alpha = jnp.exp(m_prev - m_new)
            p = jnp.exp(s - m_new)
            l_ref[c] = alpha * l_ref[c] + jnp.sum(p, axis=-1, keepdims=True)
            acc_ref[c] = alpha * acc_ref[c] + _dot(p.astype(BF16), vblk)
            m_ref[c] = m_new

    def body(j, carry):
        block(j, False)
        return carry

    lax.fori_loop(0, i, body, 0)
    block(i, True)

    lam = _diff_lambda(lam_ref, lam_init)
    o1 = acc_ref[0] / l_ref[0]
    o2 = acc_ref[1] / l_ref[1]
    o_ref[...] = _diff_finish(o1, o2, lam, sub_ref[...], lam_init).astype(o_ref.dtype)


def _flash_call(lam_vecs, sub, q_rot, k_rot, z, col_v, batch, seq, heads, lam_init, tq):
    w = 2 * HEAD_DIM
    nq = seq // tq
    kern = functools.partial(_flash_kernel, tq=tq, lam_init=lam_init)
    return pl.pallas_call(
        kern,
        out_shape=SDS((batch * seq, heads * w), BF16),
        grid=(batch, heads, nq),
        in_specs=[pl.BlockSpec((4, HEAD_DIM), lambda b, h, i: (0, 0)),
                  pl.BlockSpec((1, w), lambda b, h, i: (0, 0)),
                  pl.BlockSpec((tq, w), lambda b, h, i: (b * nq + i, h)),
                  pl.BlockSpec((seq, w), lambda b, h, i: (b, h)),
                  pl.BlockSpec((seq, w), lambda b, h, i: (b, col_v // w + h))],
        out_specs=pl.BlockSpec((tq, w), lambda b, h, i: (b * nq + i, h)),
        scratch_shapes=[pltpu.VMEM((seq, w), BF16), pltpu.VMEM((seq, w), BF16),
                        pltpu.VMEM((2, tq, 1), F32), pltpu.VMEM((2, tq, 1), F32), pltpu.VMEM((2, tq, w), F32)],
        compiler_params=_cp(("arbitrary", "arbitrary", "arbitrary")),
        name="diff_flash_prompt",
    )(lam_vecs, sub, q_rot, k_rot, z)


def _dec_attn_kernel(pt_ref, lam_ref, sub_ref, q_ref, kn_ref, vn_ref, *rest, n_steps, heads, tdec, lam_init):
    npg = PAGES_PER_STEP
    k_pages = rest[:npg]
    v_pages = rest[npg:2 * npg]
    o_ref, a_ref, b_ref, bias_ref, m_ref, l_ref, acc_ref = rest[2 * npg:]
    s_id = pl.program_id(1)
    dh = HEAD_DIM
    scale = dh ** -0.5
    pr = k_pages[0].shape[0]
    grp = 2 * heads
    nq = grp * tdec
    tshift = tdec.bit_length() - 1

    @pl.when(s_id == 0)
    def _():
        m_ref[...] = jnp.full(m_ref.shape, -jnp.inf, F32)
        l_ref[...] = jnp.zeros(l_ref.shape, F32)
        acc_ref[...] = jnp.zeros(acc_ref.shape, F32)
        r = lax.broadcasted_iota(jnp.int32, bias_ref.shape, 0)
        c = lax.broadcasted_iota(jnp.int32, bias_ref.shape, 1)
        bias_ref[...] = jnp.where((c & (grp - 1)) == (r >> tshift), 0.0, -jnp.inf)

    def swap_halves(v):
        n = v.shape[0]
        return pltpu.roll(v.reshape(n // grp, grp, dh), heads, axis=1).reshape(n, dh)

    for pg in range(npg):
        rows = slice(pg * pr, (pg + 1) * pr)
        a_ref[rows, :] = k_pages[pg][...].astype(BF16)
        vpg = v_pages[pg][...]
        b_ref[rows, 0:dh] = vpg.astype(BF16)
        b_ref[rows, dh:2 * dh] = swap_halves(vpg).astype(BF16)

    qs = q_ref[...].astype(BF16)

    def update(s, vcat):
        m_prev = m_ref[...]
        m_new = jnp.maximum(m_prev, jnp.max(s, axis=-1, keepdims=True))
        alpha = jnp.exp(m_prev - m_new)
        p = jnp.exp(s - m_new)
        l_ref[...] = alpha * l_ref[...] + jnp.sum(p, axis=-1, keepdims=True)
        acc_ref[...] = alpha * acc_ref[...] + _dot(p.astype(BF16), vcat)
        m_ref[...] = m_new

    update(_dot_nt(qs, a_ref[...]) * scale + bias_ref[...], b_ref[...])

    @pl.when(s_id == n_steps - 1)
    def _():
        nk = kn_ref.shape[0]
        r = lax.broadcasted_iota(jnp.int32, (nq, nk), 0)
        c = lax.broadcasted_iota(jnp.int32, (nq, nk), 1)
        ok = ((c & (grp - 1)) == (r >> tshift)) & ((c >> (grp.bit_length() - 1)) <= (r & (tdec - 1)))
        s_new = jnp.where(ok, _dot_nt(qs, kn_ref[...].astype(BF16)) * scale, -jnp.inf)
        vn = vn_ref[...]
        update(s_new, jnp.concatenate([vn, swap_halves(vn)], axis=1).astype(BF16))
        o = acc_ref[...] / l_ref[...]
        half = nq // 2
        o1 = o[:half]
        o2 = jnp.concatenate([o[half:, dh:], o[half:, :dh]], axis=1)
        lam = _diff_lambda(lam_ref, lam_init)
        o_ref[...] = _diff_finish(o1, o2, lam, sub_ref[...], lam_init).astype(o_ref.dtype)


def _slab_rows(x, batch, tdec, heads):
    dh = HEAD_DIM
    return x.reshape(batch, tdec, heads, 2, dh).transpose(0, 1, 3, 2, 4).reshape(batch, tdec * 2 * heads, dh)


def _dec_attn_call(page_table, lam_vecs, sub, q_rot, k_new, v_new, cache_k, cache_v, layer, tdec, lam_init):
    batch, n_pages = page_table.shape
    n_pool, depth, page, heads, w = cache_k.shape
    dh = HEAD_DIM
    grp = 2 * heads
    assert tdec & (tdec - 1) == 0 and n_pages % PAGES_PER_STEP == 0 and grp == SUBLANES and w == 2 * dh
    n_steps = n_pages // PAGES_PER_STEP
    pr = page * grp
    nq = grp * tdec
    nk_pad = LANES
    kern = functools.partial(_dec_attn_kernel, n_steps=n_steps, heads=heads, tdec=tdec, lam_init=lam_init)

    def slab_view(cache):
        return cache.reshape(n_pool, depth, page, heads, 2, dh).transpose(0, 1, 2, 4, 3, 5).reshape(n_pool, depth, pr, dh)

    q_slab = q_rot.reshape(batch, tdec, heads, 2, dh).transpose(0, 3, 2, 1, 4).reshape(batch, nq, dh)
    pad = ((0, 0), (0, nk_pad - nq), (0, 0))
    kn_slab = jnp.pad(_slab_rows(k_new, batch, tdec, heads), pad)
    vn_slab = jnp.pad(_slab_rows(v_new, batch, tdec, heads), pad)

    def page_spec(pg):
        return pl.BlockSpec((None, None, pr, dh), lambda b, s, pt: (pt[b, s * PAGES_PER_STEP + pg], layer, 0, 0))

    grid_spec = pltpu.PrefetchScalarGridSpec(
        num_scalar_prefetch=1,
        grid=(batch, n_steps),
        in_specs=[pl.BlockSpec((4, dh), lambda b, s, pt: (0, 0)),
                  pl.BlockSpec((1, w), lambda b, s, pt: (0, 0)),
                  pl.BlockSpec((None, nq, dh), lambda b, s, pt: (b, 0, 0)),
                  pl.BlockSpec((None, nk_pad, dh), lambda b, s, pt: (b, 0, 0)),
                  pl.BlockSpec((None, nk_pad, dh), lambda b, s, pt: (b, 0, 0))]
                 + [page_spec(pg) for pg in range(PAGES_PER_STEP)] * 2,
        out_specs=pl.BlockSpec((None, nq // 2, w), lambda b, s, pt: (b, 0, 0)),
        scratch_shapes=[pltpu.VMEM((PAGES_PER_STEP * pr, dh), BF16), pltpu.VMEM((PAGES_PER_STEP * pr, w), BF16),
                        pltpu.VMEM((nq, PAGES_PER_STEP * pr), F32),
                        pltpu.VMEM((nq, 1), F32), pltpu.VMEM((nq, 1), F32), pltpu.VMEM((nq, w), F32)],
    )
    o = pl.pallas_call(
        kern,
        out_shape=SDS((batch, nq // 2, w), BF16),
        grid_spec=grid_spec,
        compiler_params=_cp(("arbitrary", "arbitrary")),
        name="diff_attn_decode",
    )(page_table, lam_vecs, sub, q_slab, kn_slab, vn_slab,
      *([slab_view(cache_k)] * PAGES_PER_STEP), *([slab_view(cache_v)] * PAGES_PER_STEP))
    return o.reshape(batch, heads, tdec, w).transpose(0, 2, 1, 3).reshape(batch * tdec, heads * w)


def _causal_conv(xp_ref, x, prev, buf0, is_first, w, tin):
    @pl.when(is_first)
    def _():
        xp_ref[0:SUBLANES, :] = buf0

    @pl.when(jnp.logical_not(is_first))
    def _():
        xp_ref[0:SUBLANES, :] = prev

    xp_ref[SUBLANES:SUBLANES + tin, :] = x
    base = SUBLANES - (CONV_W - 1)
    y = xp_ref[pl.ds(base, tin), :] * w[0:1]
    for j in range(1, CONV_W):
        y = y + xp_ref[pl.ds(base + j, tin), :] * w[j:j + 1]
    return y


def _l2norm(x):
    return x * lax.rsqrt(jnp.sum(x * x, axis=-1, keepdims=True) + EPS)


def _bmm(a, b):
    return jnp.einsum('bij,bjk->bik', a, b, preferred_element_type=F32)


def _bmm_nt(a, b):
    return jnp.einsum('bik,bjk->bij', a, b, preferred_element_type=F32)


def _gdn_chunk_math(y, g, beta, u_ref, w_ref, qt_ref, kt_ref, qk_ref, eg_ref, *, heads, n_levels):
    nb, c_len, _ = y.shape
    dh = HEAD_DIM
    gw = heads * dh
    r = lax.broadcasted_iota(jnp.int32, (c_len, c_len), 0)
    c = lax.broadcasted_iota(jnp.int32, (c_len, c_len), 1)
    incl = r >= c
    strict = r > c
    tri = incl.astype(F32)
    gcs = [jnp.dot(tri, g[b], precision=lax.Precision.HIGHEST, preferred_element_type=F32) for b in range(nb)]
    gc = jnp.stack(gcs)
    gc_t = jnp.stack([x.T for x in gcs])

    for h in range(heads):
        qh = _l2norm(y[:, :, h * dh:(h + 1) * dh]) * (dh ** -0.5)
        kh = _l2norm(y[:, :, gw + h * dh:gw + (h + 1) * dh])
        vh = y[:, :, 2 * gw + h * dh:2 * gw + (h + 1) * dh]
        gcol = gc[:, :, heads + h:heads + h + 1]
        grow = gc_t[:, heads + h:heads + h + 1, :]
        bcol = beta[:, :, h:h + 1]
        glast = gc[:, c_len - 1:c_len, heads + h:heads + h + 1]
        dmat = jnp.where(incl, jnp.exp(jnp.minimum(gcol - grow, 0.0)), 0.0)
        kb = kh * bcol
        kh16 = kh.astype(BF16)
        lmat = jnp.where(strict, _bmm_nt(kb.astype(BF16), kh16) * dmat, 0.0)

        ymat = None
        for sh in range(n_levels):
            e = jnp.where(((r >> sh) == (c >> sh) + 1) & ((r >> (sh + 1)) == (c >> (sh + 1))), lmat, 0.0)
            if ymat is None:
                ymat = -e
            else:
                y16 = ymat.astype(BF16)
                f = e + _bmm(y16, e.astype(BF16))
                ymat = ymat - (f + _bmm(f.astype(BF16), y16))

        egc = jnp.exp(gcol)
        rhs = jnp.concatenate([vh * bcol, kb * egc], axis=2)
        sol = rhs + _bmm(ymat.astype(BF16), rhs.astype(BF16))
        lanes = slice(h * dh, (h + 1) * dh)
        u_ref[:, :, lanes] = sol[:, :, :dh]
        w_ref[:, :, lanes] = sol[:, :, dh:]
        qt_ref[:, :, lanes] = qh * egc
        kt_ref[:, :, lanes] = kh * jnp.exp(glast - gcol)
        qk_ref[:, :, lanes] = jnp.where(incl, _bmm_nt(qh.astype(BF16), kh16) * dmat, 0.0)
        eg_ref[:, :, lanes] = jnp.broadcast_to(jnp.exp(glast), (nb, SUBLANES, dh))


def _gdn_gates(blk, avec, dtvec):
    beta = jax.nn.sigmoid(blk)
    g = -jnp.exp(avec) * jax.nn.softplus(blk + dtvec)
    return g, beta


def _gdn_prep_seq_kernel(x_ref, prev_ref, buf0_ref, cw_ref, gba_ref, avec_ref, dtvec_ref, gz_ref,
                         u_ref, w_ref, qt_ref, kt_ref, qk_ref, eg_ref, gzo_ref, xp_ref, *, nb, tiles_per_seq, heads):
    c_len = GDN_CHUNK
    tin = nb * c_len
    gzo_ref[...] = gz_ref[...].reshape(gzo_ref.shape)
    is_first = (pl.program_id(0) % tiles_per_seq) == 0
    y = _silu(_causal_conv(xp_ref, x_ref[...], prev_ref[...], buf0_ref[...], is_first, cw_ref[...], tin))
    g, beta = _gdn_gates(gba_ref[...], avec_ref[...], dtvec_ref[...])
    _gdn_chunk_math(y.reshape(nb, c_len, y.shape[1]), g.reshape(nb, c_len, LANES), beta.reshape(nb, c_len, LANES),
                    u_ref, w_ref, qt_ref, kt_ref, qk_ref, eg_ref, heads=heads, n_levels=c_len.bit_length() - 1)


def _gdn_prep_short_kernel(x_ref, buf0_ref, cw_ref, gba_ref, avec_ref, dtvec_ref, gz_ref,
                           u_ref, w_ref, qt_ref, kt_ref, qk_ref, eg_ref, gzo_ref, xp_ref, ypad_ref, gpad_ref, bpad_ref,
                           *, nb, seq, heads):
    cw = x_ref.shape[1]
    gzo_ref[...] = jnp.zeros(gzo_ref.shape, F32)
    gzo_ref[:, 0:seq, :] = gz_ref[...].reshape(nb, seq, gzo_ref.shape[2])
    xp_ref[:, 0:SUBLANES, :] = buf0_ref[...].reshape(nb, SUBLANES, cw)
    xp_ref[:, SUBLANES:SUBLANES + seq, :] = x_ref[...].reshape(nb, seq, cw)
    base = SUBLANES - (CONV_W - 1)
    w = cw_ref[...]
    y = xp_ref[:, pl.ds(base, seq), :] * w[0:1]
    for j in range(1, CONV_W):
        y = y + xp_ref[:, pl.ds(base + j, seq), :] * w[j:j + 1]
    y = _silu(y)
    g, beta = _gdn_gates(gba_ref[...], avec_ref[...], dtvec_ref[...])
    ypad_ref[...] = jnp.zeros(ypad_ref.shape, F32)
    gpad_ref[...] = jnp.zeros(gpad_ref.shape, F32)
    bpad_ref[...] = jnp.zeros(bpad_ref.shape, F32)
    ypad_ref[:, 0:seq, :] = y
    gpad_ref[:, 0:seq, :] = g.reshape(nb, seq, LANES)
    bpad_ref[:, 0:seq, :] = beta.reshape(nb, seq, LANES)
    _gdn_chunk_math(ypad_ref[...], gpad_ref[...], bpad_ref[...], u_ref, w_ref, qt_ref, kt_ref, qk_ref, eg_ref,
                    heads=heads, n_levels=seq.bit_length() - 1)


def _gdn_prep_call(z, row0, batch, seq, buf0, conv_w, avec, dtvec, heads, nb):
    c_len = GDN_CHUNK
    gw = heads * HEAD_DIM
    cw = 3 * gw
    gba_blk = GBA_COL // LANES
    vec = pl.BlockSpec((1, LANES), lambda i: (0, 0))
    if seq >= c_len:
        tin = nb * c_len
        assert seq % tin == 0 and row0 % tin == 0
        tiles_per_seq = seq // tin
        n_tiles = batch * tiles_per_seq
        blk0 = row0 // tin
        prev0 = row0 // SUBLANES
        kern = functools.partial(_gdn_prep_seq_kernel, nb=nb, tiles_per_seq=tiles_per_seq, heads=heads)
        in_specs = [pl.BlockSpec((tin, cw), lambda i: (blk0 + i, 0)),
                    pl.BlockSpec((SUBLANES, cw), lambda i: (jnp.maximum(prev0 + i * (tin // SUBLANES) - 1, 0), 0)),
                    pl.BlockSpec((SUBLANES, cw), lambda i: (i // tiles_per_seq, 0)),
                    pl.BlockSpec((CONV_W, cw), lambda i: (0, 0)),
                    pl.BlockSpec((tin, LANES), lambda i: (blk0 + i, gba_blk)), vec, vec,
                    pl.BlockSpec((tin, gw), lambda i: (blk0 + i, COL_GZ // gw))]
        scratch = [pltpu.VMEM((tin + SUBLANES, cw), F32)]
        args = (z, z, buf0, conv_w, z, avec, dtvec, z)
    else:
        assert seq == SUBLANES and batch % nb == 0 and row0 % (nb * seq) == 0
        tin = nb * seq
        n_tiles = batch // nb
        blk0 = row0 // tin
        kern = functools.partial(_gdn_prep_short_kernel, nb=nb, seq=seq, heads=heads)
        in_specs = [pl.BlockSpec((tin, cw), lambda i: (blk0 + i, 0)),
                    pl.BlockSpec((nb * SUBLANES, cw), lambda i: (i, 0)),
                    pl.BlockSpec((CONV_W, cw), lambda i: (0, 0)),
                    pl.BlockSpec((tin, LANES), lambda i: (blk0 + i, gba_blk)), vec, vec,
                    pl.BlockSpec((tin, gw), lambda i: (blk0 + i, COL_GZ // gw))]
        scratch = [pltpu.VMEM((nb, SUBLANES + seq, cw), F32), pltpu.VMEM((nb, c_len, cw), F32),
                   pltpu.VMEM((nb, c_len, LANES), F32), pltpu.VMEM((nb, c_len, LANES), F32)]
        args = (z, buf0, conv_w, z, avec, dtvec, z)
    n_chunks = n_tiles * nb
    out_blk = pl.BlockSpec((nb, c_len, gw), lambda i: (i, 0, 0))
    return pl.pallas_call(
        kern,
        out_shape=(tuple(SDS((n_chunks, c_len, gw), F32) for _ in range(5)) + (SDS((n_chunks, SUBLANES, gw), F32),)
                   + (SDS((n_chunks, c_len, gw), F32),)),
        grid=(n_tiles,),
        in_specs=in_specs,
        out_specs=(out_blk,) * 5 + (pl.BlockSpec((nb, SUBLANES, gw), lambda i: (i, 0, 0)), out_blk),
        scratch_shapes=scratch,
        compiler_params=_cp(("arbitrary",)),
        name="gdn_prep",
    )(*args)


def _gdn_scan_kernel(u_ref, w_ref, qt_ref, kt_ref, qk_ref, eg_ref, s0_ref, gz_ref, gn_ref, o_ref, s_out_ref, s_ref,
                     *, n_c, tin, heads, nb):
    c_len = GDN_CHUNK
    dh = HEAD_DIM
    ci = pl.program_id(1)

    @pl.when(ci == 0)
    def _():
        s_ref[...] = s0_ref[...]

    gn = gn_ref[...]
    for h in range(heads):
        lanes = slice(h * dh, (h + 1) * dh)
        s_h = s_ref[:, h]
        s16 = s_h.astype(BF16)
        wq = jnp.concatenate([w_ref[:, :, lanes], qt_ref[:, :, lanes]], axis=1).astype(BF16)
        ws_qs = jnp.einsum('bck,bkv->bcv', wq, s16, preferred_element_type=F32)
        v_new = u_ref[:, :, lanes] - ws_qs[:, :c_len]
        v16 = v_new.astype(BF16)
        o = ws_qs[:, c_len:] + jnp.einsum('bij,bjv->biv', qk_ref[:, :, lanes].astype(BF16), v16,
                                          preferred_element_type=F32)
        eg = eg_ref[:, 0:1, lanes]
        s_ref[:, h] = s_h * eg + jnp.einsum('bck,bcv->bkv', kt_ref[:, :, lanes].astype(BF16), v16,
                                            preferred_element_type=F32)
        o = o[:, :tin]
        o = o * lax.rsqrt(jnp.mean(o * o, axis=-1, keepdims=True) + EPS) * gn
        o_ref[:, :, lanes] = (o * _silu(gz_ref[:, 0:tin, lanes])).astype(o_ref.dtype)

    @pl.when(ci == n_c - 1)
    def _():
        s_out_ref[...] = s_ref[...]


def _gdn_scan_call(prep, s0, gn, batch, seq, heads, nb):
    u, w, qt, kt, qk, eg, gz = prep
    c_len = GDN_CHUNK
    dh = HEAD_DIM
    gw = heads * dh
    tin = min(seq, c_len)
    n_c = max(seq // c_len, 1)
    kern = functools.partial(_gdn_scan_kernel, n_c=n_c, tin=tin, heads=heads, nb=nb)
    blk = pl.BlockSpec((nb, c_len, gw), lambda b, i: (b, i, 0))
    return pl.pallas_call(
        kern,
        out_shape=(SDS((batch, seq, gw), BF16), SDS((batch, heads, dh, dh), F32)),
        grid=(batch // nb, n_c),
        in_specs=[blk, blk, blk, blk, blk,
                  pl.BlockSpec((nb, SUBLANES, gw), lambda b, i: (b, i, 0)),
                  pl.BlockSpec((nb, heads, dh, dh), lambda b, i: (b, 0, 0, 0)),
                  blk,
                  pl.BlockSpec((1, dh), lambda b, i: (0, 0))],
        out_specs=(pl.BlockSpec((nb, tin, gw), lambda b, i: (b, i, 0)),
                   pl.BlockSpec((nb, heads, dh, dh), lambda b, i: (b, 0, 0, 0))),
        scratch_shapes=[pltpu.VMEM((nb, heads, dh, dh), F32)],
        compiler_params=_cp(("arbitrary", "arbitrary")),
        name="gdn_scan",
    )(u, w, qt, kt, qk, eg, s0, gz, gn)


def _lru_kernel(x_ref, prev_ref, buf0_ref, cw_ref, cb_ref, wa_ref, ba_ref, wx_ref, bx_ref, lam_ref, h0_ref, lg_ref,
                o_ref, hl_ref, xp_ref, hc_ref, *, tin):
    i = pl.program_id(1)

    @pl.when(i == 0)
    def _():
        hc_ref[...] = h0_ref[...]

    xc = _causal_conv(xp_ref, x_ref[...], prev_ref[...], buf0_ref[...], i == 0, cw_ref[...], tin) + cb_ref[...]
    x16 = xc.astype(BF16)
    rg = jax.nn.sigmoid(_dot(x16, wa_ref[...]) + ba_ref[...])
    ig = jax.nn.sigmoid(_dot(x16, wx_ref[...]) + bx_ref[...])
    log_a = -LRU_C * rg * jax.nn.softplus(-lam_ref[...])
    a = jnp.exp(log_a)
    t = jnp.tanh(log_a)
    b = jnp.sqrt(-2.0 * t / (1.0 - t)) * (ig * xc)

    row = lax.broadcasted_iota(jnp.int32, a.shape, 0)
    d = 1
    while d < tin:
        keep = row >= d
        a_s = jnp.where(keep, pltpu.roll(a, d, 0), 1.0)
        b_s = jnp.where(keep, pltpu.roll(b, d, 0), 0.0)
        b = a * b_s + b
        a = a * a_s
        d *= 2
    hseq = b + a * hc_ref[...]
    hc_ref[...] = hseq[tin - 1:tin, :]
    hl_ref[...] = hseq[tin - 1:tin, :]
    o_ref[...] = (hseq * jax.nn.gelu(lg_ref[...])).astype(o_ref.dtype)


def _lru_call(z, row0, batch, seq, buf0, conv_w, conv_b, wa, ba, wx, bx, lam, h0, col_x, col_g, tin):
    w = conv_w.shape[1]
    n_t = seq // tin
    assert seq == n_t * tin and row0 % tin == 0 and tin & (tin - 1) == 0
    blk0 = row0 // tin
    prev_per_tile = tin // SUBLANES
    prev0 = row0 // SUBLANES
    kern = functools.partial(_lru_kernel, tin=tin)
    vec = pl.BlockSpec((1, w), lambda b, i: (0, 0))
    return pl.pallas_call(
        kern,
        out_shape=(SDS((batch, seq, w), BF16), SDS((batch, 1, w), F32)),
        grid=(batch, n_t),
        in_specs=[pl.BlockSpec((tin, w), lambda b, i: (blk0 + b * n_t + i, col_x // w)),
                  pl.BlockSpec((SUBLANES, w),
                               lambda b, i: (jnp.maximum(prev0 + (b * n_t + i) * prev_per_tile - 1, 0), col_x // w)),
                  pl.BlockSpec((SUBLANES, w), lambda b, i: (b, 0)),
                  pl.BlockSpec((CONV_W, w), lambda b, i: (0, 0)),
                  vec,
                  pl.BlockSpec((w, w), lambda b, i: (0, 0)), vec,
                  pl.BlockSpec((w, w), lambda b, i: (0, 0)), vec,
                  vec,
                  pl.BlockSpec((None, 1, w), lambda b, i: (b, 0, 0)),
                  pl.BlockSpec((tin, w), lambda b, i: (blk0 + b * n_t + i, col_g // w))],
        out_specs=(pl.BlockSpec((None, tin, w), lambda b, i: (b, i, 0)),
                   pl.BlockSpec((None, 1, w), lambda b, i: (b, 0, 0))),
        scratch_shapes=[pltpu.VMEM((tin + SUBLANES, w), F32), pltpu.VMEM((1, w), F32)],
        compiler_params=_cp(("arbitrary", "arbitrary")),
        name="rglru",
    )(z, z, buf0, conv_w, conv_b, wa, ba, wx, bx, lam, h0, z)


GDN_W = 512
DIFF_W = 1024
LRU_W = 512
COL_GQKV = 0
COL_GZ = 3 * GDN_W
COL_DQ = COL_GZ + GDN_W
COL_DK = COL_DQ + DIFF_W
COL_DV = COL_DK + DIFF_W
COL_LX = COL_DV + DIFF_W
COL_LG = COL_LX + LRU_W
GBA_COL = COL_LG + LRU_W
NZ = 6400


def _pad_state_rows(buf):
    b, k, c = buf.shape
    return jnp.pad(buf, ((0, 0), (SUBLANES - k, 0), (0, 0))).reshape(b * SUBLANES, c)


def kernel(x_prompt, x_sample, cache_k, cache_v, state_gdn, state_gdn_conv, state_lru, state_lru_conv, page_table,
           c_prompt, c_sample, ada_w, ada_b, norm_ffn1, ffn1_w_gate, ffn1_w_up, ffn1_w_down, norm_mix, w_in,
           gdn_conv_w, gdn_a_log, gdn_dt_bias, gdn_norm, diff_lq1, diff_lk1, diff_lq2, diff_lk2, diff_subln,
           lru_conv_w, lru_conv_b, lru_wa, lru_ba, lru_wx, lru_bx, lru_lambda, w_out, norm_ffn2, ffn2_w_gate,
           ffn2_w_up, ffn2_w_down, final_norm):
    bp, tp, d = x_prompt.shape
    bd, td, _ = x_sample.shape
    depth = ada_w.shape[0]
    gdn_heads = gdn_a_log.shape[1]
    diff_heads = cache_k.shape[3]
    page = cache_k.shape[2]
    past_len = page_table.shape[1] * page
    mp_rows = bp * tp
    m = mp_rows + bd * td
    tm = math.gcd(m, ROW_TILE)
    groups_per_seq = tp // SUBLANES
    assert td == SUBLANES and gdn_heads * HEAD_DIM == GDN_W and d == 2048 and groups_per_seq & (groups_per_seq - 1) == 0
    rmap = _RowMap(groups_prompt=mp_rows // SUBLANES, shift_prompt=groups_per_seq.bit_length() - 1, n_prompt=bp)

    x = jnp.concatenate([x_prompt.reshape(mp_rows, d), x_sample.reshape(bd * td, d)], axis=0)
    n_c = bp + bd
    c_all = jnp.pad(jnp.concatenate([c_prompt, c_sample], axis=0), ((0, (-n_c) % LANES), (0, 0)))
    mod = _ada_call(c_all, ada_w, ada_b)

    zeros_gdn_buf = jnp.zeros((bp * SUBLANES, 3 * GDN_W), F32)
    zeros_lru_buf = jnp.zeros((bp * SUBLANES, LRU_W), F32)
    ffn_w = [tuple(w.astype(BF16) for w in ws) for ws in ((ffn1_w_gate, ffn1_w_up, ffn1_w_down),
                                                           (ffn2_w_gate, ffn2_w_up, ffn2_w_down))]
    w_out16 = w_out.astype(BF16)
    tail = CONV_W - 1

    def last_rows(zz, row0, bsz, seq, col, width):
        if seq <= SUBLANES:
            return zz[row0:row0 + bsz * seq, col:col + width].reshape(bsz, seq, width)[:, seq - tail:]
        return jnp.stack([zz[row0 + (b + 1) * seq - tail:row0 + (b + 1) * seq, col:col + width] for b in range(bsz)])

    outs = {k: [] for k in ("k_p", "v_p", "s_p", "sb_p", "h_p", "hb_p", "k_s", "v_s", "s_s", "sb_s", "h_s", "hb_s")}
    for l in range(depth):
        lam_init = 0.8 - 0.6 * math.exp(-0.3 * l)

        x = _ffn_call(x, norm_ffn1[l][None], mod, l, 0, *ffn_w[0], rmap, tm=tm, tf=512)

        wl = w_in[l]
        o1 = 4 * GDN_W
        o2 = o1 + 2 * gdn_heads
        w_re = jnp.concatenate([wl[:, :o1], wl[:, o2:], wl[:, o1:o2]], axis=1)
        w_re = jnp.pad(w_re, ((0, 0), (0, NZ - w_re.shape[1]))).astype(BF16)
        z = _proj_in_call(x, norm_mix[l][None], mod, l, 3, w_re, rmap, tm=tm, tn=1280)

        lam_vecs = jnp.stack([diff_lq1[l], diff_lk1[l], diff_lq2[l], diff_lk2[l]])
        sub = diff_subln[l][None]
        avec = jnp.zeros((1, LANES), F32).at[0, gdn_heads:2 * gdn_heads].set(gdn_a_log[l])
        dtvec = jnp.zeros((1, LANES), F32).at[0, gdn_heads:2 * gdn_heads].set(gdn_dt_bias[l])
        bw = lru_wa.shape[2]
        eye = jnp.eye(LRU_BLOCKS, dtype=F32)
        wa_full = (eye[:, None, :, None] * lru_wa[l][:, :, None, :]).reshape(LRU_W, LRU_W).astype(BF16)
        wx_full = (eye[:, None, :, None] * lru_wx[l][:, :, None, :]).reshape(LRU_W, LRU_W).astype(BF16)

        mixes = {}
        for path in ("p", "s"):
            if path == "p":
                row0, bsz, seq, pos0 = 0, bp, tp, 0
                gdn_buf0, lru_buf0 = zeros_gdn_buf, zeros_lru_buf
                s0 = jnp.zeros((bp, gdn_heads, HEAD_DIM, HEAD_DIM), F32)
                h0 = jnp.zeros((bp, 1, LRU_W), F32)
            else:
                row0, bsz, seq, pos0 = mp_rows, bd, td, past_len
                gdn_buf0 = _pad_state_rows(state_gdn_conv[:, l])
                lru_buf0 = _pad_state_rows(state_lru_conv[:, l])
                s0 = state_gdn[:, l]
                h0 = state_lru[:, l][:, None, :]
            rows = bsz * seq

            prep = _gdn_prep_call(z, row0, bsz, seq, gdn_buf0, gdn_conv_w[l], avec, dtvec, gdn_heads,
                                  nb=math.gcd(4, seq // GDN_CHUNK) if path == "p" else math.gcd(bsz, 8))
            prep3 = tuple(p.reshape(bsz, -1, GDN_W) for p in prep)
            o_gdn, s_new = _gdn_scan_call(prep3, s0, gdn_norm[l][None], bsz, seq, gdn_heads, nb=math.gcd(bsz, 4))

            q_rot, k_rot, v_rows = _rope_call(z, row0, rows, seq, pos0, COL_DQ, COL_DK, COL_DV, DIFF_W,
                                              tq=math.gcd(rows, 256))
            if path == "p":
                o_diff = _flash_call(lam_vecs, sub, q_rot, k_rot, z, COL_DV, bsz, seq, diff_heads, lam_init,
                                     tq=math.gcd(seq, 512))
            else:
                o_diff = _dec_attn_call(page_table, lam_vecs, sub, q_rot, k_rot, v_rows, cache_k, cache_v, l, seq,
                                        lam_init)

            o_lru, h_last = _lru_call(z, row0, bsz, seq, lru_buf0, lru_conv_w[l], lru_conv_b[l][None], wa_full,
                                      lru_ba[l][None], wx_full, lru_bx[l][None], lru_lambda[l][None], h0,
                                      COL_LX, COL_LG, tin=min(seq, 256))

            mixes[path] = (o_gdn.reshape(rows, GDN_W), o_diff.reshape(rows, DIFF_W), o_lru.reshape(rows, LRU_W))
            outs["k_" + path].append(k_rot.reshape(bsz, seq, diff_heads, 2 * HEAD_DIM))
            outs["v_" + path].append(v_rows.reshape(bsz, seq, diff_heads, 2 * HEAD_DIM))
            outs["s_" + path].append(s_new)
            outs["sb_" + path].append(last_rows(z, row0, bsz, seq, COL_GQKV, 3 * GDN_W))
            outs["h_" + path].append(h_last.reshape(bsz, LRU_W))
            outs["hb_" + path].append(last_rows(z, row0, bsz, seq, COL_LX, LRU_W))

        x = _proj_out_call(mixes["p"], mixes["s"], w_out16, l, x, mod, 5, rmap,
                           tm=math.gcd(math.gcd(mp_rows, bd * td), 256))
        x = _ffn_call(x, norm_ffn2[l][None], mod, l, 6, *ffn_w[1], rmap, tm=tm, tf=512)

    y_prompt = _final_norm_call(x, final_norm[None], 0, mp_rows, math.gcd(mp_rows, 256)).reshape(bp, tp, d)
    y_sample = _final_norm_call(x, final_norm[None], mp_rows, bd * td, math.gcd(bd * td, 256)).reshape(bd, td, d)
    st = {k: jnp.stack(v, axis=1) for k, v in outs.items()}
    return (y_prompt, y_sample, st["k_p"], st["v_p"], st["s_p"], st["sb_p"], st["h_p"], st["hb_p"],
            st["k_s"], st["v_s"], st["s_s"], st["sb_s"], st["h_s"], st["hb_s"])
```

```python
import functools
import math
from typing import NamedTuple

import jax
import jax.numpy as jnp
from jax import lax
from jax.experimental import pallas as pl
from jax.experimental.pallas import tpu as pltpu

F32 = jnp.float32
BF16 = jnp.bfloat16
SDS = jax.ShapeDtypeStruct

EPS = 1e-6
HEAD_DIM = 128
CONV_W = 4
ROPE_THETA = 10000.0
LRU_C = 8.0
N_MOD = 9
LRU_BLOCKS = 8
SUBLANES = 8
LANES = 128
VMEM_LIMIT = 60 * 1024 * 1024
GDN_CHUNK = 128
PAGES_PER_STEP = 16
ROW_TILE = 768
DEC_KEY_GROUPS = 4


def _cp(sem):
    return pltpu.CompilerParams(dimension_semantics=sem, vmem_limit_bytes=VMEM_LIMIT)


def _dot(a, b):
    return jnp.dot(a, b, preferred_element_type=F32)


def _dot_nt(a, b):
    return lax.dot_general(a, b, (((1,), (1,)), ((), ())), preferred_element_type=F32)


def _silu(x):
    return x * jax.nn.sigmoid(x)


class _RowMap(NamedTuple):
    groups_prompt: int
    shift_prompt: int
    n_prompt: int


def _group_mods(mod_ref, tile, t8, rmap):
    mp = mod_ref.shape[0]
    g = tile * t8 + lax.broadcasted_iota(jnp.int32, (t8, mp), 0)
    col = lax.broadcasted_iota(jnp.int32, (t8, mp), 1)
    src = jnp.where(g < rmap.groups_prompt, g >> rmap.shift_prompt, g - rmap.groups_prompt + rmap.n_prompt)
    return jnp.dot((col == src).astype(F32), mod_ref[...], precision=lax.Precision.HIGHEST, preferred_element_type=F32)


def _norm_mod(x, gain, shift8, scale8):
    tm, d = x.shape
    y = x * lax.rsqrt(jnp.mean(x * x, axis=-1, keepdims=True) + EPS) * gain
    y3 = y.reshape(tm // SUBLANES, SUBLANES, d)
    y3 = y3 * (1.0 + scale8[:, None, :]) + shift8[:, None, :]
    return y3.reshape(tm, d)


def _gated_residual(x, y, gate8, half):
    tm, d = x.shape
    g = gate8 * 0.5 if half else gate8
    y3 = y.reshape(tm // SUBLANES, SUBLANES, d) * g[:, None, :]
    return x + y3.reshape(tm, d)


def _ada_kernel(c_ref, w_ref, b_ref, o_ref):
    c = c_ref[...]
    o_ref[0] = _dot(_silu(c).astype(BF16), w_ref[0].astype(BF16)) + b_ref[0]


def _ada_call(c_all, ada_w, ada_b):
    n_layers, d, n = ada_w.shape
    mp = c_all.shape[0]
    tn = 1024
    return pl.pallas_call(
        _ada_kernel,
        out_shape=SDS((n_layers, mp, n), F32),
        grid=(n_layers, n // tn),
        in_specs=[pl.BlockSpec((mp, d), lambda l, j: (0, 0)),
                  pl.BlockSpec((1, d, tn), lambda l, j: (l, 0, j)),
                  pl.BlockSpec((1, 1, tn), lambda l, j: (l, 0, j))],
        out_specs=pl.BlockSpec((1, mp, tn), lambda l, j: (l, 0, j)),
        compiler_params=_cp(("arbitrary", "arbitrary")),
        name="ada_proj",
    )(c_all, ada_w, ada_b.reshape(n_layers, 1, n))


def _ffn_kernel(x_ref, gain_ref, sh_ref, sc_ref, gt_ref, wg_ref, wu_ref, wd_ref, o_ref, h_ref, *, n_j, col_chunk, rmap):
    i = pl.program_id(0)
    j = pl.program_id(1)
    t8 = x_ref.shape[0] // SUBLANES

    @pl.when(j == 0)
    def _():
        sh8 = _group_mods(sh_ref, i, t8, rmap)
        sc8 = _group_mods(sc_ref, i, t8, rmap)
        h_ref[...] = _norm_mod(x_ref[...], gain_ref[...], sh8, sc8).astype(BF16)
        o_ref[...] = jnp.zeros_like(o_ref)

    h = h_ref[...]
    g = _dot(h, wg_ref[...])
    u = _dot(h, wu_ref[...])
    a = (_silu(g) * u).astype(BF16)
    d = o_ref.shape[1]
    for c in range(0, d, col_chunk):
        o_ref[:, c:c + col_chunk] += _dot(a, wd_ref[:, c:c + col_chunk])

    @pl.when(j == n_j - 1)
    def _():
        o_ref[...] = _gated_residual(x_ref[...], o_ref[...], _group_mods(gt_ref, i, t8, rmap), half=True)


def _mod_spec(mod, layer, k):
    mp = mod.shape[1]
    d = mod.shape[2] // N_MOD
    return pl.BlockSpec((None, mp, d), lambda i, j: (layer, 0, k))


def _ffn_call(x, gain, mod, layer, k_shift, wg, wu, wd, rmap, *, tm, tf):
    m, d = x.shape
    f = wg.shape[2]
    n_j = f // tf
    kern = functools.partial(_ffn_kernel, n_j=n_j, col_chunk=512, rmap=rmap)
    return pl.pallas_call(
        kern,
        out_shape=SDS((m, d), F32),
        grid=(m // tm, n_j),
        in_specs=[pl.BlockSpec((tm, d), lambda i, j: (i, 0)),
                  pl.BlockSpec((1, d), lambda i, j: (0, 0)),
                  _mod_spec(mod, layer, k_shift), _mod_spec(mod, layer, k_shift + 1), _mod_spec(mod, layer, k_shift + 2),
                  pl.BlockSpec((None, d, tf), lambda i, j: (layer, 0, j)),
                  pl.BlockSpec((None, d, tf), lambda i, j: (layer, 0, j)),
                  pl.BlockSpec((None, tf, d), lambda i, j: (layer, j, 0))],
        out_specs=pl.BlockSpec((tm, d), lambda i, j: (i, 0)),
        scratch_shapes=[pltpu.VMEM((tm, d), BF16)],
        compiler_params=_cp(("arbitrary", "arbitrary")),
        name="ffn_swiglu",
    )(x, gain, mod, mod, mod, wg, wu, wd)


def _proj_in_kernel(x_ref, gain_ref, sh_ref, sc_ref, w_ref, o_ref, h_ref, *, rmap):
    @pl.when(pl.program_id(1) == 0)
    def _():
        i = pl.program_id(0)
        t8 = x_ref.shape[0] // SUBLANES
        sh8 = _group_mods(sh_ref, i, t8, rmap)
        sc8 = _group_mods(sc_ref, i, t8, rmap)
        h_ref[...] = _norm_mod(x_ref[...], gain_ref[...], sh8, sc8).astype(BF16)

    o_ref[...] = _dot(h_ref[...], w_ref[...])


def _proj_in_call(x, gain, mod, layer, k_shift, w, rmap, *, tm, tn):
    m, d = x.shape
    n = w.shape[1]
    return pl.pallas_call(
        functools.partial(_proj_in_kernel, rmap=rmap),
        out_shape=SDS((m, n), F32),
        grid=(m // tm, n // tn),
        in_specs=[pl.BlockSpec((tm, d), lambda i, j: (i, 0)),
                  pl.BlockSpec((1, d), lambda i, j: (0, 0)),
                  _mod_spec(mod, layer, k_shift), _mod_spec(mod, layer, k_shift + 1),
                  pl.BlockSpec((d, tn), lambda i, j: (0, j))],
        out_specs=pl.BlockSpec((tm, tn), lambda i, j: (i, j)),
        scratch_shapes=[pltpu.VMEM((tm, d), BF16)],
        compiler_params=_cp(("arbitrary", "arbitrary")),
        name="proj_in",
    )(x, gain, mod, mod, w)


def _proj_out_kernel(*refs, n_mix, n_prompt_tiles, rmap):
    prompt = refs[:n_mix]
    decode = refs[n_mix:2 * n_mix]
    w_ref, x_ref, gt_ref, o_ref = refs[2 * n_mix:]
    i = pl.program_id(0)
    t8 = x_ref.shape[0] // SUBLANES

    def run(parts):
        k0 = 0
        y = None
        for a_ref in parts:
            kw = a_ref.shape[1]
            part = _dot(a_ref[...], w_ref[k0:k0 + kw, :])
            y = part if y is None else y + part
            k0 += kw
        o_ref[...] = _gated_residual(x_ref[...], y, _group_mods(gt_ref, i, t8, rmap), half=False)

    @pl.when(i < n_prompt_tiles)
    def _():
        run(prompt)

    @pl.when(i >= n_prompt_tiles)
    def _():
        run(decode)


def _proj_out_call(prompt_parts, decode_parts, w, layer, x, mod, k_gate, rmap, *, tm):
    m, d = x.shape
    rows_p = prompt_parts[0].shape[0]
    assert rows_p % tm == 0 and (m - rows_p) % tm == 0
    npt = rows_p // tm
    n_mix = len(prompt_parts)
    kern = functools.partial(_proj_out_kernel, n_mix=n_mix, n_prompt_tiles=npt, rmap=rmap)
    p_specs = [pl.BlockSpec((tm, a.shape[1]), lambda i, j: (jnp.minimum(i, npt - 1), 0)) for a in prompt_parts]
    s_specs = [pl.BlockSpec((tm, a.shape[1]), lambda i, j: (jnp.maximum(i - npt, 0), 0)) for a in decode_parts]
    return pl.pallas_call(
        kern,
        out_shape=SDS((m, d), F32),
        grid=(m // tm, 1),
        in_specs=p_specs + s_specs + [pl.BlockSpec((None, w.shape[1], d), lambda i, j: (layer, 0, 0)),
                                      pl.BlockSpec((tm, d), lambda i, j: (i, 0)),
                                      _mod_spec(mod, layer, k_gate)],
        out_specs=pl.BlockSpec((tm, d), lambda i, j: (i, 0)),
        compiler_params=_cp(("arbitrary", "arbitrary")),
        name="proj_out",
    )(*prompt_parts, *decode_parts, w, x, mod)


def _final_norm_kernel(x_ref, g_ref, o_ref):
    x = x_ref[...]
    o_ref[...] = x * lax.rsqrt(jnp.mean(x * x, axis=-1, keepdims=True) + EPS) * g_ref[...]


def _final_norm_call(x, gain, row0, rows, tm):
    d = x.shape[1]
    blk0 = row0 // tm
    return pl.pallas_call(
        _final_norm_kernel,
        out_shape=SDS((rows, d), F32),
        grid=(rows // tm,),
        in_specs=[pl.BlockSpec((tm, d), lambda i: (blk0 + i, 0)),
                  pl.BlockSpec((1, d), lambda i: (0, 0))],
        out_specs=pl.BlockSpec((tm, d), lambda i: (i, 0)),
        compiler_params=_cp(("arbitrary",)),
        name="final_norm",
    )(x, gain)


def _rope_kernel(q_ref, k_ref, v_ref, *rest, tq, seq, pos0, slab_heads):
    i = pl.program_id(0)
    if slab_heads:
        qo_ref, ko_ref, kso_ref, vso_ref = rest[-4:]
    else:
        qo_ref, ko_ref, vo_ref = rest
        vo_ref[...] = v_ref[...]
    half = HEAD_DIM // 2
    row = i * tq + lax.broadcasted_iota(jnp.int32, (tq, HEAD_DIM), 0)
    lane = lax.broadcasted_iota(jnp.int32, (tq, HEAD_DIM), 1)
    pos = (pos0 + (row & (seq - 1))).astype(F32)
    freq = (lane & (half - 1)).astype(F32)
    inv = jnp.float32(ROPE_THETA) ** (-freq / half)
    ang = pos * inv
    cos = jnp.cos(ang)
    sin = jnp.where(lane < half, -jnp.sin(ang), jnp.sin(ang))
    grp = 2 * slab_heads

    def slab_rows(g):
        return pl.ds((g % 2) * slab_heads + g // 2, tq, stride=grp)

    for g in range(q_ref.shape[1] // HEAD_DIM):
        lanes = slice(g * HEAD_DIM, (g + 1) * HEAD_DIM)
        x = q_ref[:, lanes]
        qo_ref[:, lanes] = x * cos + pltpu.roll(x, half, 1) * sin
        x = k_ref[:, lanes]
        y = x * cos + pltpu.roll(x, half, 1) * sin
        ko_ref[:, lanes] = y
        if slab_heads:
            kso_ref[slab_rows(g), :] = y
            vso_ref[slab_rows(g), :] = v_ref[:, lanes]


def _rope_call(z, row0, rows, seq, pos0, col_q, col_k, col_v, width, tq, slab=None):
    assert seq & (seq - 1) == 0 and rows % tq == 0 and row0 % tq == 0
    blk0 = row0 // tq
    in_specs = [pl.BlockSpec((tq, width), lambda i: (blk0 + i, col_q // width)),
                pl.BlockSpec((tq, width), lambda i: (blk0 + i, col_k // width)),
                pl.BlockSpec((tq, width), lambda i: (blk0 + i, col_v // width))]
    out_blk = pl.BlockSpec((tq, width), lambda i: (i, 0))
    args = [z, z, z]
    aliases = {}
    if slab is None:
        slab_heads = 0
        out_shape = (SDS((rows, width), F32),) * 3
        out_specs = (out_blk,) * 3
    else:
        layer, depth, slab_heads, prev = slab
        grp = 2 * slab_heads
        tiles_per_seq = seq // tq
        assert seq % tq == 0 and width == grp * HEAD_DIM
        slab_sds = SDS((rows // seq, depth, seq * grp, HEAD_DIM), F32)
        slab_blk = pl.BlockSpec((None, None, tq * grp, HEAD_DIM),
                                lambda i: (i // tiles_per_seq, layer, i % tiles_per_seq, 0))
        out_shape = (SDS((rows, width), F32),) * 2 + (slab_sds,) * 2
        out_specs = (out_blk,) * 2 + (slab_blk,) * 2
        if prev is not None:
            in_specs += [pl.BlockSpec(memory_space=pl.ANY)] * 2
            args += list(prev)
            aliases = {3: 2, 4: 3}
    kern = functools.partial(_rope_kernel, tq=tq, seq=seq, pos0=pos0, slab_heads=slab_heads)
    return pl.pallas_call(
        kern,
        out_shape=out_shape,
        grid=(rows // tq,),
        in_specs=in_specs,
        out_specs=out_specs,
        input_output_aliases=aliases,
        compiler_params=_cp(("arbitrary",)),
        name="rope_qk",
    )(*args)


def _diff_lambda(lam_ref, lam_init):
    v = lam_ref[...]
    s1 = jnp.sum(v[0:1] * v[1:2], axis=-1, keepdims=True)
    s2 = jnp.sum(v[2:3] * v[3:4], axis=-1, keepdims=True)
    return jnp.exp(s1) - jnp.exp(s2) + lam_init


def _diff_finish(o1, o2, lam, sub, lam_init):
    o = o1 - lam * o2
    o = o * lax.rsqrt(jnp.mean(o * o, axis=-1, keepdims=True) + EPS) * sub
    return o * (1.0 - lam_init)


def _flash_kernel(lam_ref, sub_ref, q_ref, k_ref, v_ref, o_ref, kb_ref, vb_ref, m_ref, l_ref, acc_ref,
                  *, tq, lam_init):
    i = pl.program_id(2)
    dh = HEAD_DIM
    scale = dh ** -0.5

    @pl.when(i == 0)
    def _():
        kb_ref[...] = k_ref[...].astype(BF16)
        vb_ref[...] = v_ref[...].astype(BF16)

    q = q_ref[...] * scale
    qs = (q[:, :dh].astype(BF16), q[:, dh:].astype(BF16))
    m_ref[...] = jnp.full(m_ref.shape, -jnp.inf, F32)
    l_ref[...] = jnp.zeros(l_ref.shape, F32)
    acc_ref[...] = jnp.zeros(acc_ref.shape, F32)

    def block(j, masked):
        start = pl.multiple_of(j * tq, tq)
        kblk = kb_ref[pl.ds(start, tq), :]
        vblk = vb_ref[pl.ds(start, tq), :]
        for c in range(2):
            s = _dot_nt(qs[c], kblk[:, c * dh:(c + 1) * dh])
            if masked:
                r = lax.broadcasted_iota(jnp.int32, s.shape, 0)
                cc = lax.broadcasted_iota(jnp.int32, s.shape, 1)
                s = jnp.where(cc <= r, s, -jnp.inf)
            m_prev = m_ref[c]
            m_new = jnp.maximum(m_prev, jnp.max(s, axis=-1, keepdims=True))
            alpha = jnp.exp(m_prev - m_new)
            p = jnp.exp(s - m_new)
            l_ref[c] = alpha * l_ref[c] + jnp.sum(p, axis=-1, keepdims=True)
            acc_ref[c] = alpha * acc_ref[c] + _dot(p.astype(BF16), vblk)
            m_ref[c] = m_new

    def body(j, carry):
        block(j, False)
        return carry

    lax.fori_loop(0, i, body, 0)
    block(i, True)

    lam = _diff_lambda(lam_ref, lam_init)
    o1 = acc_ref[0] / l_ref[0]
    o2 = acc_ref[1] / l_ref[1]
    o_ref[...] = _diff_finish(o1, o2, lam, sub_ref[...], lam_init).astype(o_ref.dtype)


def _flash_call(lam_vecs, sub, q_rot, k_rot, z, col_v, batch, seq, heads, lam_init, tq):
    w = 2 * HEAD_DIM
    nq = seq // tq
    kern = functools.partial(_flash_kernel, tq=tq, lam_init=lam_init)
    return pl.pallas_call(
        kern,
        out_shape=SDS((batch * seq, heads * w), BF16),
        grid=(batch, heads, nq),
        in_specs=[pl.BlockSpec((4, HEAD_DIM), lambda b, h, i: (0, 0)),
                  pl.BlockSpec((1, w), lambda b, h, i: (0, 0)),
                  pl.BlockSpec((tq, w), lambda b, h, i: (b * nq + i, h)),
                  pl.BlockSpec((seq, w), lambda b, h, i: (b, h)),
                  pl.BlockSpec((seq, w), lambda b, h, i: (b, col_v // w + h))],
        out_specs=pl.BlockSpec((tq, w), lambda b, h, i: (b * nq + i, h)),
        scratch_shapes=[pltpu.VMEM((seq, w), BF16), pltpu.VMEM((seq, w), BF16),
                        pltpu.VMEM((2, tq, 1), F32), pltpu.VMEM((2, tq, 1), F32), pltpu.VMEM((2, tq, w), F32)],
        compiler_params=_cp(("arbitrary", "arbitrary", "arbitrary")),
        name="diff_flash_prompt",
    )(lam_vecs, sub, q_rot, k_rot, z)


def _dec_attn_kernel(pt_ref, lam_ref, sub_ref, q_ref, kn_ref, vn_ref, *rest, n_steps, heads, tdec, lam_init):
    npg = PAGES_PER_STEP
    k_pages = rest[:npg]
    v_pages = rest[npg:2 * npg]
    o_ref, a_ref, b_ref, bias_ref, m_ref, l_ref, acc_ref = rest[2 * npg:]
    s_id = pl.program_id(1)
    dh = HEAD_DIM
    scale = dh ** -0.5
    pr = k_pages[0].shape[0]
    grp = 2 * heads
    nq = grp * tdec
    tshift = tdec.bit_length() - 1

    @pl.when(s_id == 0)
    def _():
        m_ref[...] = jnp.full(m_ref.shape, -jnp.inf, F32)
        l_ref[...] = jnp.zeros(l_ref.shape, F32)
        acc_ref[...] = jnp.zeros(acc_ref.shape, F32)
        r = lax.broadcasted_iota(jnp.int32, bias_ref.shape, 0)
        c = lax.broadcasted_iota(jnp.int32, bias_ref.shape, 1)
        bias_ref[...] = jnp.where((c & (grp - 1)) == (r >> tshift), 0.0, -jnp.inf)

    def swap_halves(v):
        n = v.shape[0]
        return pltpu.roll(v.reshape(n // grp, grp, dh), heads, axis=1).reshape(n, dh)

    qs = (q_ref[...] * scale).astype(BF16)

    def partial_softmax(s, vcat):
        m_g = jnp.max(s, axis=-1, keepdims=True)
        p = jnp.exp(s - m_g)
        return m_g, jnp.sum(p, axis=-1, keepdims=True), _dot(p.astype(BF16), vcat)

    def merge(parts):
        m_prev = m_ref[...]
        m_new = m_prev
        for m_g, _, _ in parts:
            m_new = jnp.maximum(m_new, m_g)
        alpha = jnp.exp(m_prev - m_new)
        l_new = alpha * l_ref[...]
        acc_new = alpha * acc_ref[...]
        for m_g, l_g, acc_g in parts:
            w_g = jnp.exp(m_g - m_new)
            l_new = l_new + w_g * l_g
            acc_new = acc_new + w_g * acc_g
        l_ref[...] = l_new
        acc_ref[...] = acc_new
        m_ref[...] = m_new

    ppg = npg // DEC_KEY_GROUPS
    bias = bias_ref[...]
    parts = []
    for g in range(DEC_KEY_GROUPS):
        for pg in range(g * ppg, (g + 1) * ppg):
            rows = slice(pg * pr, (pg + 1) * pr)
            a_ref[rows, :] = k_pages[pg][...].astype(BF16)
            vpg = v_pages[pg][...]
            b_ref[rows, 0:dh] = vpg.astype(BF16)
            b_ref[rows, dh:2 * dh] = swap_halves(vpg).astype(BF16)
        rows = slice(g * ppg * pr, (g + 1) * ppg * pr)
        parts.append(partial_softmax(_dot_nt(qs, a_ref[rows, :]) + bias, b_ref[rows, :]))
    merge(parts)

    @pl.when(s_id == n_steps - 1)
    def _():
        nk = kn_ref.shape[0]
        r = lax.broadcasted_iota(jnp.int32, (nq, nk), 0)
        c = lax.broadcasted_iota(jnp.int32, (nq, nk), 1)
        ok = ((c & (grp - 1)) == (r >> tshift)) & ((c >> (grp.bit_length() - 1)) <= (r & (tdec - 1)))
        s_new = jnp.where(ok, _dot_nt(qs, kn_ref[...].astype(BF16)), -jnp.inf)
        vn = vn_ref[...]
        merge([partial_softmax(s_new, jnp.concatenate([vn, swap_halves(vn)], axis=1).astype(BF16))])
        o = acc_ref[...] / l_ref[...]
        half = nq // 2
        o1 = o[:half]
        o2 = jnp.concatenate([o[half:, dh:], o[half:, :dh]], axis=1)
        lam = _diff_lambda(lam_ref, lam_init)
        o_ref[...] = _diff_finish(o1, o2, lam, sub_ref[...], lam_init).astype(o_ref.dtype)


def _slab_rows(x, batch, tdec, heads):
    dh = HEAD_DIM
    return x.reshape(batch, tdec, heads, 2, dh).transpose(0, 1, 3, 2, 4).reshape(batch, tdec * 2 * heads, dh)


def _dec_attn_call(page_table, lam_vecs, sub, q_rot, k_new, v_new, cache_k, cache_v, layer, tdec, lam_init):
    batch, n_pages = page_table.shape
    n_pool, depth, page, heads, w = cache_k.shape
    dh = HEAD_DIM
    grp = 2 * heads
    assert tdec & (tdec - 1) == 0 and n_pages % PAGES_PER_STEP == 0 and grp == SUBLANES and w == 2 * dh
    n_steps = n_pages // PAGES_PER_STEP
    pr = page * grp
    nq = grp * tdec
    nk_pad = LANES
    kern = functools.partial(_dec_attn_kernel, n_steps=n_steps, heads=heads, tdec=tdec, lam_init=lam_init)

    def slab_view(cache):
        return cache.reshape(n_pool, depth, page, heads, 2, dh).transpose(0, 1, 2, 4, 3, 5).reshape(n_pool, depth, pr, dh)

    q_slab = q_rot.reshape(batch, tdec, heads, 2, dh).transpose(0, 3, 2, 1, 4).reshape(batch, nq, dh)
    pad = ((0, 0), (0, nk_pad - nq), (0, 0))
    kn_slab = jnp.pad(_slab_rows(k_new, batch, tdec, heads), pad)
    vn_slab = jnp.pad(_slab_rows(v_new, batch, tdec, heads), pad)

    def page_spec(pg):
        return pl.BlockSpec((None, None, pr, dh), lambda b, s, pt: (pt[b, s * PAGES_PER_STEP + pg], layer, 0, 0))

    grid_spec = pltpu.PrefetchScalarGridSpec(
        num_scalar_prefetch=1,
        grid=(batch, n_steps),
        in_specs=[pl.BlockSpec((4, dh), lambda b, s, pt: (0, 0)),
                  pl.BlockSpec((1, w), lambda b, s, pt: (0, 0)),
                  pl.BlockSpec((None, nq, dh), lambda b, s, pt: (b, 0, 0)),
                  pl.BlockSpec((None, nk_pad, dh), lambda b, s, pt: (b, 0, 0)),
                  pl.BlockSpec((None, nk_pad, dh), lambda b, s, pt: (b, 0, 0))]
                 + [page_spec(pg) for pg in range(PAGES_PER_STEP)] * 2,
        out_specs=pl.BlockSpec((None, nq // 2, w), lambda b, s, pt: (b, 0, 0)),
        scratch_shapes=[pltpu.VMEM((PAGES_PER_STEP * pr, dh), BF16), pltpu.VMEM((PAGES_PER_STEP * pr, w), BF16),
                        pltpu.VMEM((nq, PAGES_PER_STEP // DEC_KEY_GROUPS * pr), F32),
                        pltpu.VMEM((nq, 1), F32), pltpu.VMEM((nq, 1), F32), pltpu.VMEM((nq, w), F32)],
    )
    o = pl.pallas_call(
        kern,
        out_shape=SDS((batch, nq // 2, w), BF16),
        grid_spec=grid_spec,
        compiler_params=_cp(("arbitrary", "arbitrary")),
        name="diff_attn_decode",
    )(page_table, lam_vecs, sub, q_slab, kn_slab, vn_slab,
      *([slab_view(cache_k)] * PAGES_PER_STEP), *([slab_view(cache_v)] * PAGES_PER_STEP))
    return o.reshape(batch, heads, tdec, w).transpose(0, 2, 1, 3).reshape(batch * tdec, heads * w)


def _causal_conv(xp_ref, x, prev, buf0, is_first, w, tin):
    @pl.when(is_first)
    def _():
        xp_ref[0:SUBLANES, :] = buf0

    @pl.when(jnp.logical_not(is_first))
    def _():
        xp_ref[0:SUBLANES, :] = prev

    xp_ref[SUBLANES:SUBLANES + tin, :] = x
    base = SUBLANES - (CONV_W - 1)
    y = xp_ref[pl.ds(base, tin), :] * w[0:1]
    for j in range(1, CONV_W):
        y = y + xp_ref[pl.ds(base + j, tin), :] * w[j:j + 1]
    return y


def _l2norm(x):
    return x * lax.rsqrt(jnp.sum(x * x, axis=-1, keepdims=True) + EPS)


def _bmm(a, b):
    return jnp.einsum('bij,bjk->bik', a, b, preferred_element_type=F32)


def _bmm_nt(a, b):
    return jnp.einsum('bik,bjk->bij', a, b, preferred_element_type=F32)


def _gdn_chunk_math(y, g, beta, u_ref, w_ref, qt_ref, kt_ref, qk_ref, eg_ref, *, heads, n_levels):
    nb, c_len, _ = y.shape
    dh = HEAD_DIM
    gw = heads * dh
    r = lax.broadcasted_iota(jnp.int32, (c_len, c_len), 0)
    c = lax.broadcasted_iota(jnp.int32, (c_len, c_len), 1)
    incl = r >= c
    strict = r > c
    tri = incl.astype(F32)
    gcs = [jnp.dot(tri, g[b], precision=lax.Precision.HIGHEST, preferred_element_type=F32) for b in range(nb)]
    gc = jnp.stack(gcs)
    gc_t = jnp.stack([x.T for x in gcs])

    for h in range(heads):
        qh = _l2norm(y[:, :, h * dh:(h + 1) * dh]) * (dh ** -0.5)
        kh = _l2norm(y[:, :, gw + h * dh:gw + (h + 1) * dh])
        vh = y[:, :, 2 * gw + h * dh:2 * gw + (h + 1) * dh]
        gcol = gc[:, :, heads + h:heads + h + 1]
        grow = gc_t[:, heads + h:heads + h + 1, :]
        bcol = beta[:, :, h:h + 1]
        glast = gc[:, c_len - 1:c_len, heads + h:heads + h + 1]
        dmat = jnp.where(incl, jnp.exp(jnp.minimum(gcol - grow, 0.0)), 0.0)
        kb = kh * bcol
        kh16 = kh.astype(BF16)
        lmat = jnp.where(strict, _bmm_nt(kb.astype(BF16), kh16) * dmat, 0.0)

        ymat = None
        for sh in range(n_levels):
            e = jnp.where(((r >> sh) == (c >> sh) + 1) & ((r >> (sh + 1)) == (c >> (sh + 1))), lmat, 0.0)
            if ymat is None:
                ymat = -e
            else:
                y16 = ymat.astype(BF16)
                f = e + _bmm(y16, e.astype(BF16))
                ymat = ymat - (f + _bmm(f.astype(BF16), y16))

        egc = jnp.exp(gcol)
        rhs = jnp.concatenate([vh * bcol, kb * egc], axis=2)
        sol = rhs + _bmm(ymat.astype(BF16), rhs.astype(BF16))
        lanes = slice(h * dh, (h + 1) * dh)
        u_ref[:, :, lanes] = sol[:, :, :dh]
        w_ref[:, :, lanes] = sol[:, :, dh:]
        qt_ref[:, :, lanes] = qh * egc
        kt_ref[:, :, lanes] = kh * jnp.exp(glast - gcol)
        qk_ref[:, :, lanes] = jnp.where(incl, _bmm_nt(qh.astype(BF16), kh16) * dmat, 0.0)
        eg_ref[:, :, lanes] = jnp.broadcast_to(jnp.exp(glast), (nb, SUBLANES, dh))


def _gdn_gates(blk, avec, dtvec):
    beta = jax.nn.sigmoid(blk)
    g = -jnp.exp(avec) * jax.nn.softplus(blk + dtvec)
    return g, beta


def _gdn_prep_seq_kernel(x_ref, prev_ref, buf0_ref, cw_ref, gba_ref, avec_ref, dtvec_ref, gz_ref,
                         u_ref, w_ref, qt_ref, kt_ref, qk_ref, eg_ref, gzo_ref, xp_ref, *, nb, tiles_per_seq, heads):
    c_len = GDN_CHUNK
    tin = nb * c_len
    gzo_ref[...] = gz_ref[...].reshape(gzo_ref.shape)
    is_first = (pl.program_id(0) % tiles_per_seq) == 0
    y = _silu(_causal_conv(xp_ref, x_ref[...], prev_ref[...], buf0_ref[...], is_first, cw_ref[...], tin))
    g, beta = _gdn_gates(gba_ref[...], avec_ref[...], dtvec_ref[...])
    _gdn_chunk_math(y.reshape(nb, c_len, y.shape[1]), g.reshape(nb, c_len, LANES), beta.reshape(nb, c_len, LANES),
                    u_ref, w_ref, qt_ref, kt_ref, qk_ref, eg_ref, heads=heads, n_levels=c_len.bit_length() - 1)


def _gdn_prep_short_kernel(x_ref, buf0_ref, cw_ref, gba_ref, avec_ref, dtvec_ref, gz_ref,
                           u_ref, w_ref, qt_ref, kt_ref, qk_ref, eg_ref, gzo_ref, xp_ref, ypad_ref, gpad_ref, bpad_ref,
                           *, nb, seq, heads):
    cw = x_ref.shape[1]
    gzo_ref[...] = jnp.zeros(gzo_ref.shape, F32)
    gzo_ref[:, 0:seq, :] = gz_ref[...].reshape(nb, seq, gzo_ref.shape[2])
    xp_ref[:, 0:SUBLANES, :] = buf0_ref[...].reshape(nb, SUBLANES, cw)
    xp_ref[:, SUBLANES:SUBLANES + seq, :] = x_ref[...].reshape(nb, seq, cw)
    base = SUBLANES - (CONV_W - 1)
    w = cw_ref[...]
    y = xp_ref[:, pl.ds(base, seq), :] * w[0:1]
    for j in range(1, CONV_W):
        y = y + xp_ref[:, pl.ds(base + j, seq), :] * w[j:j + 1]
    y = _silu(y)
    g, beta = _gdn_gates(gba_ref[...], avec_ref[...], dtvec_ref[...])
    ypad_ref[...] = jnp.zeros(ypad_ref.shape, F32)
    gpad_ref[...] = jnp.zeros(gpad_ref.shape, F32)
    bpad_ref[...] = jnp.zeros(bpad_ref.shape, F32)
    ypad_ref[:, 0:seq, :] = y
    gpad_ref[:, 0:seq, :] = g.reshape(nb, seq, LANES)
    bpad_ref[:, 0:seq, :] = beta.reshape(nb, seq, LANES)
    _gdn_chunk_math(ypad_ref[...], gpad_ref[...], bpad_ref[...], u_ref, w_ref, qt_ref, kt_ref, qk_ref, eg_ref,
                    heads=heads, n_levels=seq.bit_length() - 1)


def _gdn_prep_call(z, row0, batch, seq, buf0, conv_w, avec, dtvec, heads, nb):
    c_len = GDN_CHUNK
    gw = heads * HEAD_DIM
    cw = 3 * gw
    gba_blk = GBA_COL // LANES
    vec = pl.BlockSpec((1, LANES), lambda i: (0, 0))
    if seq >= c_len:
        tin = nb * c_len
        assert seq % tin == 0 and row0 % tin == 0
        tiles_per_seq = seq // tin
        n_tiles = batch * tiles_per_seq
        blk0 = row0 // tin
        prev0 = row0 // SUBLANES
        kern = functools.partial(_gdn_prep_seq_kernel, nb=nb, tiles_per_seq=tiles_per_seq, heads=heads)
        in_specs = [pl.BlockSpec((tin, cw), lambda i: (blk0 + i, 0)),
                    pl.BlockSpec((SUBLANES, cw), lambda i: (jnp.maximum(prev0 + i * (tin // SUBLANES) - 1, 0), 0)),
                    pl.BlockSpec((SUBLANES, cw), lambda i: (i // tiles_per_seq, 0)),
                    pl.BlockSpec((CONV_W, cw), lambda i: (0, 0)),
                    pl.BlockSpec((tin, LANES), lambda i: (blk0 + i, gba_blk)), vec, vec,
                    pl.BlockSpec((tin, gw), lambda i: (blk0 + i, COL_GZ // gw))]
        scratch = [pltpu.VMEM((tin + SUBLANES, cw), F32)]
        args = (z, z, buf0, conv_w, z, avec, dtvec, z)
    else:
        assert seq == SUBLANES and batch % nb == 0 and row0 % (nb * seq) == 0
        tin = nb * seq
        n_tiles = batch // nb
        blk0 = row0 // tin
        kern = functools.partial(_gdn_prep_short_kernel, nb=nb, seq=seq, heads=heads)
        in_specs = [pl.BlockSpec((tin, cw), lambda i: (blk0 + i, 0)),
                    pl.BlockSpec((nb * SUBLANES, cw), lambda i: (i, 0)),
                    pl.BlockSpec((CONV_W, cw), lambda i: (0, 0)),
                    pl.BlockSpec((tin, LANES), lambda i: (blk0 + i, gba_blk)), vec, vec,
                    pl.BlockSpec((tin, gw), lambda i: (blk0 + i, COL_GZ // gw))]
        scratch = [pltpu.VMEM((nb, SUBLANES + seq, cw), F32), pltpu.VMEM((nb, c_len, cw), F32),
                   pltpu.VMEM((nb, c_len, LANES), F32), pltpu.VMEM((nb, c_len, LANES), F32)]
        args = (z, buf0, conv_w, z, avec, dtvec, z)
    n_chunks = n_tiles * nb
    out_blk = pl.BlockSpec((nb, c_len, gw), lambda i: (i, 0, 0))
    return pl.pallas_call(
        kern,
        out_shape=(tuple(SDS((n_chunks, c_len, gw), F32) for _ in range(5)) + (SDS((n_chunks, SUBLANES, gw), F32),)
                   + (SDS((n_chunks, c_len, gw), F32),)),
        grid=(n_tiles,),
        in_specs=in_specs,
        out_specs=(out_blk,) * 5 + (pl.BlockSpec((nb, SUBLANES, gw), lambda i: (i, 0, 0)), out_blk),
        scratch_shapes=scratch,
        compiler_params=_cp(("arbitrary",)),
        name="gdn_prep",
    )(*args)


def _gdn_scan_kernel(u_ref, w_ref, qt_ref, kt_ref, qk_ref, eg_ref, s0_ref, gz_ref, gn_ref, o_ref, s_out_ref, s_ref,
                     *, n_c, tin, heads, nb):
    c_len = GDN_CHUNK
    dh = HEAD_DIM
    ci = pl.program_id(1)

    @pl.when(ci == 0)
    def _():
        s_ref[...] = s0_ref[...]

    gn = gn_ref[...]
    for h in range(heads):
        lanes = slice(h * dh, (h + 1) * dh)
        s_h = s_ref[:, h]
        s16 = s_h.astype(BF16)
        wq = jnp.concatenate([w_ref[:, :, lanes], qt_ref[:, :, lanes]], axis=1).astype(BF16)
        ws_qs = jnp.einsum('bck,bkv->bcv', wq, s16, preferred_element_type=F32)
        v_new = u_ref[:, :, lanes] - ws_qs[:, :c_len]
        v16 = v_new.astype(BF16)
        o = ws_qs[:, c_len:] + jnp.einsum('bij,bjv->biv', qk_ref[:, :, lanes].astype(BF16), v16,
                                          preferred_element_type=F32)
        eg = eg_ref[:, 0:1, lanes]
        s_ref[:, h] = s_h * eg + jnp.einsum('bck,bcv->bkv', kt_ref[:, :, lanes].astype(BF16), v16,
                                            preferred_element_type=F32)
        o = o[:, :tin]
        o = o * lax.rsqrt(jnp.mean(o * o, axis=-1, keepdims=True) + EPS) * gn
        o_ref[:, :, lanes] = (o * _silu(gz_ref[:, 0:tin, lanes])).astype(o_ref.dtype)

    @pl.when(ci == n_c - 1)
    def _():
        s_out_ref[...] = s_ref[...]


def _gdn_scan_call(prep, s0, gn, batch, seq, heads, nb):
    u, w, qt, kt, qk, eg, gz = prep
    c_len = GDN_CHUNK
    dh = HEAD_DIM
    gw = heads * dh
    tin = min(seq, c_len)
    n_c = max(seq // c_len, 1)
    kern = functools.partial(_gdn_scan_kernel, n_c=n_c, tin=tin, heads=heads, nb=nb)
    blk = pl.BlockSpec((nb, c_len, gw), lambda b, i: (b, i, 0))
    return pl.pallas_call(
        kern,
        out_shape=(SDS((batch, seq, gw), BF16), SDS((batch, heads, dh, dh), F32)),
        grid=(batch // nb, n_c),
        in_specs=[blk, blk, blk, blk, blk,
                  pl.BlockSpec((nb, SUBLANES, gw), lambda b, i: (b, i, 0)),
                  pl.BlockSpec((nb, heads, dh, dh), lambda b, i: (b, 0, 0, 0)),
                  blk,
                  pl.BlockSpec((1, dh), lambda b, i: (0, 0))],
        out_specs=(pl.BlockSpec((nb, tin, gw), lambda b, i: (b, i, 0)),
                   pl.BlockSpec((nb, heads, dh, dh), lambda b, i: (b, 0, 0, 0))),
        scratch_shapes=[pltpu.VMEM((nb, heads, dh, dh), F32)],
        compiler_params=_cp(("arbitrary", "arbitrary")),
        name="gdn_scan",
    )(u, w, qt, kt, qk, eg, s0, gz, gn)


def _lru_kernel(x_ref, prev_ref, buf0_ref, cw_ref, cb_ref, wa_ref, ba_ref, wx_ref, bx_ref, lam_ref, h0_ref, lg_ref,
                o_ref, hl_ref, xp_ref, hc_ref, *, tin):
    i = pl.program_id(1)

    @pl.when(i == 0)
    def _():
        hc_ref[...] = h0_ref[...]

    xc = _causal_conv(xp_ref, x_ref[...], prev_ref[...], buf0_ref[...], i == 0, cw_ref[...], tin) + cb_ref[...]
    x16 = xc.astype(BF16)
    rg = jax.nn.sigmoid(_dot(x16, wa_ref[...]) + ba_ref[...])
    ig = jax.nn.sigmoid(_dot(x16, wx_ref[...]) + bx_ref[...])
    log_a = -LRU_C * rg * jax.nn.softplus(-lam_ref[...])
    a = jnp.exp(log_a)
    t = jnp.tanh(log_a)
    b = jnp.sqrt(-2.0 * t / (1.0 - t)) * (ig * xc)

    row = lax.broadcasted_iota(jnp.int32, a.shape, 0)
    d = 1
    while d < tin:
        keep = row >= d
        a_s = jnp.where(keep, pltpu.roll(a, d, 0), 1.0)
        b_s = jnp.where(keep, pltpu.roll(b, d, 0), 0.0)
        b = a * b_s + b
        a = a * a_s
        d *= 2
    hseq = b + a * hc_ref[...]
    hc_ref[...] = hseq[tin - 1:tin, :]
    hl_ref[...] = hseq[tin - 1:tin, :]
    o_ref[...] = (hseq * jax.nn.gelu(lg_ref[...])).astype(o_ref.dtype)


def _lru_call(z, row0, batch, seq, buf0, conv_w, conv_b, wa, ba, wx, bx, lam, h0, col_x, col_g, tin):
    w = conv_w.shape[1]
    n_t = seq // tin
    assert seq == n_t * tin and row0 % tin == 0 and tin & (tin - 1) == 0
    blk0 = row0 // tin
    prev_per_tile = tin // SUBLANES
    prev0 = row0 // SUBLANES
    kern = functools.partial(_lru_kernel, tin=tin)
    vec = pl.BlockSpec((1, w), lambda b, i: (0, 0))
    return pl.pallas_call(
        kern,
        out_shape=(SDS((batch, seq, w), BF16), SDS((batch, 1, w), F32)),
        grid=(batch, n_t),
        in_specs=[pl.BlockSpec((tin, w), lambda b, i: (blk0 + b * n_t + i, col_x // w)),
                  pl.BlockSpec((SUBLANES, w),
                               lambda b, i: (jnp.maximum(prev0 + (b * n_t + i) * prev_per_tile - 1, 0), col_x // w)),
                  pl.BlockSpec((SUBLANES, w), lambda b, i: (b, 0)),
                  pl.BlockSpec((CONV_W, w), lambda b, i: (0, 0)),
                  vec,
                  pl.BlockSpec((w, w), lambda b, i: (0, 0)), vec,
                  pl.BlockSpec((w, w), lambda b, i: (0, 0)), vec,
                  vec,
                  pl.BlockSpec((None, 1, w), lambda b, i: (b, 0, 0)),
                  pl.BlockSpec((tin, w), lambda b, i: (blk0 + b * n_t + i, col_g // w))],
        out_specs=(pl.BlockSpec((None, tin, w), lambda b, i: (b, i, 0)),
                   pl.BlockSpec((None, 1, w), lambda b, i: (b, 0, 0))),
        scratch_shapes=[pltpu.VMEM((tin + SUBLANES, w), F32), pltpu.VMEM((1, w), F32)],
        compiler_params=_cp(("arbitrary", "arbitrary")),
        name="rglru",
    )(z, z, buf0, conv_w, conv_b, wa, ba, wx, bx, lam, h0, z)


GDN_W = 512
DIFF_W = 1024
LRU_W = 512
COL_GQKV = 0
COL_GZ = 3 * GDN_W
COL_DQ = COL_GZ + GDN_W
COL_DK = COL_DQ + DIFF_W
COL_DV = COL_DK + DIFF_W
COL_LX = COL_DV + DIFF_W
COL_LG = COL_LX + LRU_W
GBA_COL = COL_LG + LRU_W
NZ = 6400


def _pad_state_rows(buf):
    b, k, c = buf.shape
    return jnp.pad(buf, ((0, 0), (SUBLANES - k, 0), (0, 0))).reshape(b * SUBLANES, c)


def kernel(x_prompt, x_sample, cache_k, cache_v, state_gdn, state_gdn_conv, state_lru, state_lru_conv, page_table,
           c_prompt, c_sample, ada_w, ada_b, norm_ffn1, ffn1_w_gate, ffn1_w_up, ffn1_w_down, norm_mix, w_in,
           gdn_conv_w, gdn_a_log, gdn_dt_bias, gdn_norm, diff_lq1, diff_lk1, diff_lq2, diff_lk2, diff_subln,
           lru_conv_w, lru_conv_b, lru_wa, lru_ba, lru_wx, lru_bx, lru_lambda, w_out, norm_ffn2, ffn2_w_gate,
           ffn2_w_up, ffn2_w_down, final_norm):
    bp, tp, d = x_prompt.shape
    bd, td, _ = x_sample.shape
    depth = ada_w.shape[0]
    gdn_heads = gdn_a_log.shape[1]
    diff_heads = cache_k.shape[3]
    page = cache_k.shape[2]
    past_len = page_table.shape[1] * page
    mp_rows = bp * tp
    m = mp_rows + bd * td
    tm = math.gcd(m, ROW_TILE)
    groups_per_seq = tp // SUBLANES
    assert td == SUBLANES and gdn_heads * HEAD_DIM == GDN_W and d == 2048 and groups_per_seq & (groups_per_seq - 1) == 0
    rmap = _RowMap(groups_prompt=mp_rows // SUBLANES, shift_prompt=groups_per_seq.bit_length() - 1, n_prompt=bp)

    x = jnp.concatenate([x_prompt.reshape(mp_rows, d), x_sample.reshape(bd * td, d)], axis=0)
    n_c = bp + bd
    c_all = jnp.pad(jnp.concatenate([c_prompt, c_sample], axis=0), ((0, (-n_c) % LANES), (0, 0)))
    mod = _ada_call(c_all, ada_w, ada_b)

    zeros_gdn_buf = jnp.zeros((bp * SUBLANES, 3 * GDN_W), F32)
    zeros_lru_buf = jnp.zeros((bp * SUBLANES, LRU_W), F32)
    ffn_w = [tuple(w.astype(BF16) for w in ws) for ws in ((ffn1_w_gate, ffn1_w_up, ffn1_w_down),
                                                           (ffn2_w_gate, ffn2_w_up, ffn2_w_down))]
    w_out16 = w_out.astype(BF16)
    tail = CONV_W - 1

    def last_rows(zz, row0, bsz, seq, col, width):
        if seq <= SUBLANES:
            return zz[row0:row0 + bsz * seq, col:col + width].reshape(bsz, seq, width)[:, seq - tail:]
        return jnp.stack([zz[row0 + (b + 1) * seq - tail:row0 + (b + 1) * seq, col:col + width] for b in range(bsz)])

    outs = {k: [] for k in ("k_p", "v_p", "s_p", "sb_p", "h_p", "hb_p", "k_s", "v_s", "s_s", "sb_s", "h_s", "hb_s")}
    prompt_kv = None
    for l in range(depth):
        lam_init = 0.8 - 0.6 * math.exp(-0.3 * l)

        x = _ffn_call(x, norm_ffn1[l][None], mod, l, 0, *ffn_w[0], rmap, tm=tm, tf=512)

        wl = w_in[l]
        o1 = 4 * GDN_W
        o2 = o1 + 2 * gdn_heads
        w_re = jnp.concatenate([wl[:, :o1], wl[:, o2:], wl[:, o1:o2]], axis=1)
        w_re = jnp.pad(w_re, ((0, 0), (0, NZ - w_re.shape[1]))).astype(BF16)
        z = _proj_in_call(x, norm_mix[l][None], mod, l, 3, w_re, rmap, tm=tm, tn=1280)

        lam_vecs = jnp.stack([diff_lq1[l], diff_lk1[l], diff_lq2[l], diff_lk2[l]])
        sub = diff_subln[l][None]
        avec = jnp.zeros((1, LANES), F32).at[0, gdn_heads:2 * gdn_heads].set(gdn_a_log[l])
        dtvec = jnp.zeros((1, LANES), F32).at[0, gdn_heads:2 * gdn_heads].set(gdn_dt_bias[l])
        bw = lru_wa.shape[2]
        eye = jnp.eye(LRU_BLOCKS, dtype=F32)
        wa_full = (eye[:, None, :, None] * lru_wa[l][:, :, None, :]).reshape(LRU_W, LRU_W).astype(BF16)
        wx_full = (eye[:, None, :, None] * lru_wx[l][:, :, None, :]).reshape(LRU_W, LRU_W).astype(BF16)

        mixes = {}
        for path in ("p", "s"):
            if path == "p":
                row0, bsz, seq, pos0 = 0, bp, tp, 0
                gdn_buf0, lru_buf0 = zeros_gdn_buf, zeros_lru_buf
                s0 = jnp.zeros((bp, gdn_heads, HEAD_DIM, HEAD_DIM), F32)
                h0 = jnp.zeros((bp, 1, LRU_W), F32)
            else:
                row0, bsz, seq, pos0 = mp_rows, bd, td, past_len
                gdn_buf0 = _pad_state_rows(state_gdn_conv[:, l])
                lru_buf0 = _pad_state_rows(state_lru_conv[:, l])
                s0 = state_gdn[:, l]
                h0 = state_lru[:, l][:, None, :]
            rows = bsz * seq

            prep = _gdn_prep_call(z, row0, bsz, seq, gdn_buf0, gdn_conv_w[l], avec, dtvec, gdn_heads,
                                  nb=math.gcd(4, seq // GDN_CHUNK) if path == "p" else math.gcd(bsz, 8))
            prep3 = tuple(p.reshape(bsz, -1, GDN_W) for p in prep)
            o_gdn, s_new = _gdn_scan_call(prep3, s0, gdn_norm[l][None], bsz, seq, gdn_heads, nb=math.gcd(bsz, 4))

            if path == "p":
                q_rot, k_rot, *prompt_kv = _rope_call(z, row0, rows, seq, pos0, COL_DQ, COL_DK, COL_DV, DIFF_W,
                                                      tq=math.gcd(seq, 256), slab=(l, depth, diff_heads, prompt_kv))
                o_diff = _flash_call(lam_vecs, sub, q_rot, k_rot, z, COL_DV, bsz, seq, diff_heads, lam_init,
                                     tq=math.gcd(seq, 512))
            else:
                q_rot, k_rot, v_rows = _rope_call(z, row0, rows, seq, pos0, COL_DQ, COL_DK, COL_DV, DIFF_W,
                                                  tq=math.gcd(rows, 256))
                o_diff = _dec_attn_call(page_table, lam_vecs, sub, q_rot, k_rot, v_rows, cache_k, cache_v, l, seq,
                                        lam_init)
                outs["k_s"].append(k_rot.reshape(bsz, seq, diff_heads, 2 * HEAD_DIM))
                outs["v_s"].append(v_rows.reshape(bsz, seq, diff_heads, 2 * HEAD_DIM))

            o_lru, h_last = _lru_call(z, row0, bsz, seq, lru_buf0, lru_conv_w[l], lru_conv_b[l][None], wa_full,
                                      lru_ba[l][None], wx_full, lru_bx[l][None], lru_lambda[l][None], h0,
                                      COL_LX, COL_LG, tin=min(seq, 256))

            mixes[path] = (o_gdn.reshape(rows, GDN_W), o_diff.reshape(rows, DIFF_W), o_lru.reshape(rows, LRU_W))
            outs["s_" + path].append(s_new)
            outs["sb_" + path].append(last_rows(z, row0, bsz, seq, COL_GQKV, 3 * GDN_W))
            outs["h_" + path].append(h_last.reshape(bsz, LRU_W))
            outs["hb_" + path].append(last_rows(z, row0, bsz, seq, COL_LX, LRU_W))

        x = _proj_out_call(mixes["p"], mixes["s"], w_out16, l, x, mod, 5, rmap,
                           tm=math.gcd(math.gcd(mp_rows, bd * td), 256))
        x = _ffn_call(x, norm_ffn2[l][None], mod, l, 6, *ffn_w[1], rmap, tm=tm, tf=512)

    y_prompt = _final_norm_call(x, final_norm[None], 0, mp_rows, math.gcd(mp_rows, 256)).reshape(bp, tp, d)
    y_sample = _final_norm_call(x, final_norm[None], mp_rows, bd * td, math.gcd(bd * td, 256)).reshape(bd, td, d)
    st = {k: jnp.stack(v, axis=1) for k, v in outs.items() if v}

    def from_slab(s):
        return (s.reshape(bp, depth, tp, 2, diff_heads, HEAD_DIM).transpose(0, 1, 2, 4, 3, 5)
                .reshape(bp, depth, tp, diff_heads, 2 * HEAD_DIM))

    st["k_p"], st["v_p"] = (from_slab(s) for s in prompt_kv)
    return (y_prompt, y_sample, st["k_p"], st["v_p"], st["s_p"], st["sb_p"], st["h_p"], st["hb_p"],
            st["k_s"], st["v_s"], st["s_s"], st["sb_s"], st["h_s"], st["hb_s"])
```

```python
import functools
import math
from typing import NamedTuple

import jax
import jax.numpy as jnp
from jax import lax
from jax.experimental import pallas as pl
from jax.experimental.pallas import tpu as pltpu

F32 = jnp.float32
BF16 = jnp.bfloat16
SDS = jax.ShapeDtypeStruct

EPS = 1e-6
HEAD_DIM = 128
CONV_W = 4
ROPE_THETA = 10000.0
LRU_C = 8.0
N_MOD = 9
LRU_BLOCKS = 8
SUBLANES = 8
LANES = 128
VMEM_LIMIT = 60 * 1024 * 1024
GDN_CHUNK = 128
PAGES_PER_STEP = 16
ROW_TILE = 768
DEC_KEY_GROUPS = 4


def _cp(sem):
    return pltpu.CompilerParams(dimension_semantics=sem, vmem_limit_bytes=VMEM_LIMIT)


def _dot(a, b):
    return jnp.dot(a, b, preferred_element_type=F32)


def _dot_nt(a, b):
    return lax.dot_general(a, b, (((1,), (1,)), ((), ())), preferred_element_type=F32)


def _silu(x):
    return x * jax.nn.sigmoid(x)


class _RowMap(NamedTuple):
    groups_prompt: int
    shift_prompt: int
    n_prompt: int


def _group_mods(mod_ref, tile, t8, rmap):
    mp = mod_ref.shape[0]
    g = tile * t8 + lax.broadcasted_iota(jnp.int32, (t8, mp), 0)
    col = lax.broadcasted_iota(jnp.int32, (t8, mp), 1)
    src = jnp.where(g < rmap.groups_prompt, g >> rmap.shift_prompt, g - rmap.groups_prompt + rmap.n_prompt)
    return jnp.dot((col == src).astype(F32), mod_ref[...], precision=lax.Precision.HIGHEST, preferred_element_type=F32)


def _norm_mod(x, gain, shift8, scale8):
    tm, d = x.shape
    y = x * lax.rsqrt(jnp.mean(x * x, axis=-1, keepdims=True) + EPS) * gain
    y3 = y.reshape(tm // SUBLANES, SUBLANES, d)
    y3 = y3 * (1.0 + scale8[:, None, :]) + shift8[:, None, :]
    return y3.reshape(tm, d)


def _gated_residual(x, y, gate8, half):
    tm, d = x.shape
    g = gate8 * 0.5 if half else gate8
    y3 = y.reshape(tm // SUBLANES, SUBLANES, d) * g[:, None, :]
    return x + y3.reshape(tm, d)


def _ada_kernel(c_ref, w_ref, b_ref, o_ref):
    c = c_ref[...]
    o_ref[0] = _dot(_silu(c).astype(BF16), w_ref[0].astype(BF16)) + b_ref[0]


def _ada_call(c_all, ada_w, ada_b):
    n_layers, d, n = ada_w.shape
    mp = c_all.shape[0]
    tn = 1024
    return pl.pallas_call(
        _ada_kernel,
        out_shape=SDS((n_layers, mp, n), F32),
        grid=(n_layers, n // tn),
        in_specs=[pl.BlockSpec((mp, d), lambda l, j: (0, 0)),
                  pl.BlockSpec((1, d, tn), lambda l, j: (l, 0, j)),
                  pl.BlockSpec((1, 1, tn), lambda l, j: (l, 0, j))],
        out_specs=pl.BlockSpec((1, mp, tn), lambda l, j: (l, 0, j)),
        compiler_params=_cp(("arbitrary", "arbitrary")),
        name="ada_proj",
    )(c_all, ada_w, ada_b.reshape(n_layers, 1, n))


def _ffn_kernel(x_ref, gain_ref, sh_ref, sc_ref, gt_ref, wg_ref, wu_ref, wd_ref, *rest, n_j, col_chunk, rmap, tile0,
                emit_weights):
    if emit_weights:
        o_ref, wgo_ref, wuo_ref, wdo_ref, h_ref = rest
    else:
        o_ref, h_ref = rest[-2:]
    i = pl.program_id(0) + tile0
    j = pl.program_id(1)
    t8 = x_ref.shape[0] // SUBLANES

    @pl.when(j == 0)
    def _():
        sh8 = _group_mods(sh_ref, i, t8, rmap)
        sc8 = _group_mods(sc_ref, i, t8, rmap)
        h_ref[...] = _norm_mod(x_ref[...], gain_ref[...], sh8, sc8).astype(BF16)
        o_ref[...] = jnp.zeros_like(o_ref)

    h = h_ref[...]
    wg, wu, wd = wg_ref[...], wu_ref[...], wd_ref
    if emit_weights:
        wg, wu, wd = wg.astype(BF16), wu.astype(BF16), wd[...].astype(BF16)
        wgo_ref[...] = wg
        wuo_ref[...] = wu
        wdo_ref[...] = wd
    g = _dot(h, wg)
    u = _dot(h, wu)
    a = (_silu(g) * u).astype(BF16)
    d = o_ref.shape[1]
    for c in range(0, d, col_chunk):
        o_ref[:, c:c + col_chunk] += _dot(a, wd[:, c:c + col_chunk])

    @pl.when(j == n_j - 1)
    def _():
        o_ref[...] = _gated_residual(x_ref[...], o_ref[...], _group_mods(gt_ref, i, t8, rmap), half=True)


def _mod_spec(mod, layer, k):
    mp = mod.shape[1]
    d = mod.shape[2] // N_MOD
    return pl.BlockSpec((None, mp, d), lambda i, j: (layer, 0, k))


def _ffn_call(x, gain, mod, layer, k_shift, wg, wu, wd, rmap, *, tm, tf, tf_first):
    m, d = x.shape
    f = wg.shape[2]
    mods = [_mod_spec(mod, layer, k_shift + k) for k in range(3)]
    common = dict(col_chunk=512, rmap=rmap)
    first = pl.pallas_call(
        functools.partial(_ffn_kernel, n_j=f // tf_first, tile0=0, emit_weights=True, **common),
        out_shape=(SDS((m, d), F32), SDS((d, f), BF16), SDS((d, f), BF16), SDS((f, d), BF16)),
        grid=(1, f // tf_first),
        in_specs=[pl.BlockSpec((tm, d), lambda i, j: (0, 0)),
                  pl.BlockSpec((1, d), lambda i, j: (0, 0))] + mods
                 + [pl.BlockSpec((None, d, tf_first), lambda i, j: (layer, 0, j)),
                    pl.BlockSpec((None, d, tf_first), lambda i, j: (layer, 0, j)),
                    pl.BlockSpec((None, tf_first, d), lambda i, j: (layer, j, 0))],
        out_specs=(pl.BlockSpec((tm, d), lambda i, j: (0, 0)),
                   pl.BlockSpec((d, tf_first), lambda i, j: (0, j)),
                   pl.BlockSpec((d, tf_first), lambda i, j: (0, j)),
                   pl.BlockSpec((tf_first, d), lambda i, j: (j, 0))),
        scratch_shapes=[pltpu.VMEM((tm, d), BF16)],
        compiler_params=_cp(("arbitrary", "arbitrary")),
        name="ffn_swiglu_first",
    )
    out0, wg16, wu16, wd16 = first(x, gain, mod, mod, mod, wg, wu, wd)
    if m == tm:
        return out0
    rest = pl.pallas_call(
        functools.partial(_ffn_kernel, n_j=f // tf, tile0=1, emit_weights=False, **common),
        out_shape=SDS((m, d), F32),
        grid=(m // tm - 1, f // tf),
        in_specs=[pl.BlockSpec((tm, d), lambda i, j: (i + 1, 0)),
                  pl.BlockSpec((1, d), lambda i, j: (0, 0))] + mods
                 + [pl.BlockSpec((d, tf), lambda i, j: (0, j)),
                    pl.BlockSpec((d, tf), lambda i, j: (0, j)),
                    pl.BlockSpec((tf, d), lambda i, j: (j, 0)),
                    pl.BlockSpec(memory_space=pl.ANY)],
        out_specs=pl.BlockSpec((tm, d), lambda i, j: (i + 1, 0)),
        scratch_shapes=[pltpu.VMEM((tm, d), BF16)],
        input_output_aliases={8: 0},
        compiler_params=_cp(("arbitrary", "arbitrary")),
        name="ffn_swiglu",
    )
    return rest(x, gain, mod, mod, mod, wg16, wu16, wd16, out0)


def _proj_in_kernel(x_ref, gain_ref, sh_ref, sc_ref, w_ref, o_ref, h_ref, *, rmap):
    @pl.when(pl.program_id(1) == 0)
    def _():
        i = pl.program_id(0)
        t8 = x_ref.shape[0] // SUBLANES
        sh8 = _group_mods(sh_ref, i, t8, rmap)
        sc8 = _group_mods(sc_ref, i, t8, rmap)
        h_ref[...] = _norm_mod(x_ref[...], gain_ref[...], sh8, sc8).astype(BF16)

    o_ref[...] = _dot(h_ref[...], w_ref[...])


def _proj_in_call(x, gain, mod, layer, k_shift, w, rmap, *, tm, tn):
    m, d = x.shape
    n = w.shape[1]
    return pl.pallas_call(
        functools.partial(_proj_in_kernel, rmap=rmap),
        out_shape=SDS((m, n), F32),
        grid=(m // tm, n // tn),
        in_specs=[pl.BlockSpec((tm, d), lambda i, j: (i, 0)),
                  pl.BlockSpec((1, d), lambda i, j: (0, 0)),
                  _mod_spec(mod, layer, k_shift), _mod_spec(mod, layer, k_shift + 1),
                  pl.BlockSpec((d, tn), lambda i, j: (0, j))],
        out_specs=pl.BlockSpec((tm, tn), lambda i, j: (i, j)),
        scratch_shapes=[pltpu.VMEM((tm, d), BF16)],
        compiler_params=_cp(("arbitrary", "arbitrary")),
        name="proj_in",
    )(x, gain, mod, mod, w)


def _proj_out_kernel(*refs, n_mix, n_prompt_tiles, rmap):
    prompt = refs[:n_mix]
    decode = refs[n_mix:2 * n_mix]
    w_ref, x_ref, gt_ref, o_ref = refs[2 * n_mix:]
    i = pl.program_id(0)
    t8 = x_ref.shape[0] // SUBLANES

    def run(parts):
        k0 = 0
        y = None
        for a_ref in parts:
            kw = a_ref.shape[1]
            part = _dot(a_ref[...], w_ref[k0:k0 + kw, :])
            y = part if y is None else y + part
            k0 += kw
        o_ref[...] = _gated_residual(x_ref[...], y, _group_mods(gt_ref, i, t8, rmap), half=False)

    @pl.when(i < n_prompt_tiles)
    def _():
        run(prompt)

    @pl.when(i >= n_prompt_tiles)
    def _():
        run(decode)


def _proj_out_call(prompt_parts, decode_parts, w, layer, x, mod, k_gate, rmap, *, tm):
    m, d = x.shape
    rows_p = prompt_parts[0].shape[0]
    assert rows_p % tm == 0 and (m - rows_p) % tm == 0
    npt = rows_p // tm
    n_mix = len(prompt_parts)
    kern = functools.partial(_proj_out_kernel, n_mix=n_mix, n_prompt_tiles=npt, rmap=rmap)
    p_specs = [pl.BlockSpec((tm, a.shape[1]), lambda i, j: (jnp.minimum(i, npt - 1), 0)) for a in prompt_parts]
    s_specs = [pl.BlockSpec((tm, a.shape[1]), lambda i, j: (jnp.maximum(i - npt, 0), 0)) for a in decode_parts]
    return pl.pallas_call(
        kern,
        out_shape=SDS((m, d), F32),
        grid=(m // tm, 1),
        in_specs=p_specs + s_specs + [pl.BlockSpec((None, w.shape[1], d), lambda i, j: (layer, 0, 0)),
                                      pl.BlockSpec((tm, d), lambda i, j: (i, 0)),
                                      _mod_spec(mod, layer, k_gate)],
        out_specs=pl.BlockSpec((tm, d), lambda i, j: (i, 0)),
        compiler_params=_cp(("arbitrary", "arbitrary")),
        name="proj_out",
    )(*prompt_parts, *decode_parts, w, x, mod)


def _final_norm_kernel(x_ref, g_ref, o_ref):
    x = x_ref[...]
    o_ref[...] = x * lax.rsqrt(jnp.mean(x * x, axis=-1, keepdims=True) + EPS) * g_ref[...]


def _final_norm_call(x, gain, row0, rows, tm):
    d = x.shape[1]
    blk0 = row0 // tm
    return pl.pallas_call(
        _final_norm_kernel,
        out_shape=SDS((rows, d), F32),
        grid=(rows // tm,),
        in_specs=[pl.BlockSpec((tm, d), lambda i: (blk0 + i, 0)),
                  pl.BlockSpec((1, d), lambda i: (0, 0))],
        out_specs=pl.BlockSpec((tm, d), lambda i: (i, 0)),
        compiler_params=_cp(("arbitrary",)),
        name="final_norm",
    )(x, gain)


def _rope_kernel(q_ref, k_ref, v_ref, *rest, tq, seq, pos0, slab_heads):
    i = pl.program_id(0)
    if slab_heads:
        qo_ref, ko_ref, kso_ref, vso_ref = rest[-4:]
    else:
        qo_ref, ko_ref, vo_ref = rest
        vo_ref[...] = v_ref[...]
    half = HEAD_DIM // 2
    row = i * tq + lax.broadcasted_iota(jnp.int32, (tq, HEAD_DIM), 0)
    lane = lax.broadcasted_iota(jnp.int32, (tq, HEAD_DIM), 1)
    pos = (pos0 + (row & (seq - 1))).astype(F32)
    freq = (lane & (half - 1)).astype(F32)
    inv = jnp.float32(ROPE_THETA) ** (-freq / half)
    ang = pos * inv
    cos = jnp.cos(ang)
    sin = jnp.where(lane < half, -jnp.sin(ang), jnp.sin(ang))
    grp = 2 * slab_heads

    def slab_rows(g):
        return pl.ds((g % 2) * slab_heads + g // 2, tq, stride=grp)

    for g in range(q_ref.shape[1] // HEAD_DIM):
        lanes = slice(g * HEAD_DIM, (g + 1) * HEAD_DIM)
        x = q_ref[:, lanes]
        qo_ref[:, lanes] = x * cos + pltpu.roll(x, half, 1) * sin
        x = k_ref[:, lanes]
        y = x * cos + pltpu.roll(x, half, 1) * sin
        ko_ref[:, lanes] = y
        if slab_heads:
            kso_ref[slab_rows(g), :] = y
            vso_ref[slab_rows(g), :] = v_ref[:, lanes]


def _rope_call(z, row0, rows, seq, pos0, col_q, col_k, col_v, width, tq, slab=None):
    assert seq & (seq - 1) == 0 and rows % tq == 0 and row0 % tq == 0
    blk0 = row0 // tq
    in_specs = [pl.BlockSpec((tq, width), lambda i: (blk0 + i, col_q // width)),
                pl.BlockSpec((tq, width), lambda i: (blk0 + i, col_k // width)),
                pl.BlockSpec((tq, width), lambda i: (blk0 + i, col_v // width))]
    out_blk = pl.BlockSpec((tq, width), lambda i: (i, 0))
    args = [z, z, z]
    aliases = {}
    if slab is None:
        slab_heads = 0
        out_shape = (SDS((rows, width), F32),) * 3
        out_specs = (out_blk,) * 3
    else:
        layer, depth, slab_heads, prev = slab
        grp = 2 * slab_heads
        tiles_per_seq = seq // tq
        assert seq % tq == 0 and width == grp * HEAD_DIM
        slab_sds = SDS((rows // seq, depth, seq * grp, HEAD_DIM), F32)
        slab_blk = pl.BlockSpec((None, None, tq * grp, HEAD_DIM),
                                lambda i: (i // tiles_per_seq, layer, i % tiles_per_seq, 0))
        out_shape = (SDS((rows, width), F32),) * 2 + (slab_sds,) * 2
        out_specs = (out_blk,) * 2 + (slab_blk,) * 2
        if prev is not None:
            in_specs += [pl.BlockSpec(memory_space=pl.ANY)] * 2
            args += list(prev)
            aliases = {3: 2, 4: 3}
    kern = functools.partial(_rope_kernel, tq=tq, seq=seq, pos0=pos0, slab_heads=slab_heads)
    return pl.pallas_call(
        kern,
        out_shape=out_shape,
        grid=(rows // tq,),
        in_specs=in_specs,
        out_specs=out_specs,
        input_output_aliases=aliases,
        compiler_params=_cp(("arbitrary",)),
        name="rope_qk",
    )(*args)


def _diff_lambda(lam_ref, lam_init):
    v = lam_ref[...]
    s1 = jnp.sum(v[0:1] * v[1:2], axis=-1, keepdims=True)
    s2 = jnp.sum(v[2:3] * v[3:4], axis=-1, keepdims=True)
    return jnp.exp(s1) - jnp.exp(s2) + lam_init


def _diff_finish(o1, o2, lam, sub, lam_init):
    o = o1 - lam * o2
    o = o * lax.rsqrt(jnp.mean(o * o, axis=-1, keepdims=True) + EPS) * sub
    return o * (1.0 - lam_init)


def _flash_kernel(lam_ref, sub_ref, q_ref, k_ref, v_ref, o_ref, kb_ref, vb_ref, m_ref, l_ref, acc_ref,
                  *, tq, lam_init):
    i = pl.program_id(2)
    dh = HEAD_DIM
    scale = dh ** -0.5

    @pl.when(i == 0)
    def _():
        kb_ref[...] = k_ref[...].astype(BF16)
        vb_ref[...] = v_ref[...].astype(BF16)

    q = q_ref[...] * scale
    qs = (q[:, :dh].astype(BF16), q[:, dh:].astype(BF16))
    m_ref[...] = jnp.full(m_ref.shape, -jnp.inf, F32)
    l_ref[...] = jnp.zeros(l_ref.shape, F32)
    acc_ref[...] = jnp.zeros(acc_ref.shape, F32)

    def block(j, masked):
        start = pl.multiple_of(j * tq, tq)
        kblk = kb_ref[pl.ds(start, tq), :]
        vblk = vb_ref[pl.ds(start, tq), :]
        for c in range(2):
            s = _dot_nt(qs[c], kblk[:, c * dh:(c + 1) * dh])
            if masked:
                r = lax.broadcasted_iota(jnp.int32, s.shape, 0)
                cc = lax.broadcasted_iota(jnp.int32, s.shape, 1)
                s = jnp.where(cc <= r, s, -jnp.inf)
            m_prev = m_ref[c]
            m_new = jnp.maximum(m_prev, jnp.max(s, axis=-1, keepdims=True))
            alpha = jnp.exp(m_prev - m_new)
            p = jnp.exp(s - m_new)
            l_ref[c] = alpha * l_ref[c] + jnp.sum(p, axis=-1, keepdims=True)
            acc_ref[c] = alpha * acc_ref[c] + _dot(p.astype(BF16), vblk)
            m_ref[c] = m_new

    def body(j, carry):
        block(j, False)
        return carry

    lax.fori_loop(0, i, body, 0)
    block(i, True)

    lam = _diff_lambda(lam_ref, lam_init)
    o1 = acc_ref[0] / l_ref[0]
    o2 = acc_ref[1] / l_ref[1]
    o_ref[...] = _diff_finish(o1, o2, lam, sub_ref[...], lam_init).astype(o_ref.dtype)


def _flash_call(lam_vecs, sub, q_rot, k_rot, z, col_v, batch, seq, heads, lam_init, tq):
    w = 2 * HEAD_DIM
    nq = seq // tq
    kern = functools.partial(_flash_kernel, tq=tq, lam_init=lam_init)
    return pl.pallas_call(
        kern,
        out_shape=SDS((batch * seq, heads * w), BF16),
        grid=(batch, heads, nq),
        in_specs=[pl.BlockSpec((4, HEAD_DIM), lambda b, h, i: (0, 0)),
                  pl.BlockSpec((1, w), lambda b, h, i: (0, 0)),
                  pl.BlockSpec((tq, w), lambda b, h, i: (b * nq + i, h)),
                  pl.BlockSpec((seq, w), lambda b, h, i: (b, h)),
                  pl.BlockSpec((seq, w), lambda b, h, i: (b, col_v // w + h))],
        out_specs=pl.BlockSpec((tq, w), lambda b, h, i: (b * nq + i, h)),
        scratch_shapes=[pltpu.VMEM((seq, w), BF16), pltpu.VMEM((seq, w), BF16),
                        pltpu.VMEM((2, tq, 1), F32), pltpu.VMEM((2, tq, 1), F32), pltpu.VMEM((2, tq, w), F32)],
        compiler_params=_cp(("arbitrary", "arbitrary", "arbitrary")),
        name="diff_flash_prompt",
    )(lam_vecs, sub, q_rot, k_rot, z)


def _dec_attn_kernel(pt_ref, lam_ref, sub_ref, q_ref, kn_ref, vn_ref, *rest, n_steps, heads, tdec, lam_init):
    npg = PAGES_PER_STEP
    k_pages = rest[:npg]
    v_pages = rest[npg:2 * npg]
    o_ref, a_ref, b_ref, bias_ref, m_ref, l_ref, acc_ref = rest[2 * npg:]
    s_id = pl.program_id(1)
    dh = HEAD_DIM
    scale = dh ** -0.5
    pr = k_pages[0].shape[0]
    grp = 2 * heads
    nq = grp * tdec
    tshift = tdec.bit_length() - 1

    @pl.when(s_id == 0)
    def _():
        m_ref[...] = jnp.full(m_ref.shape, -jnp.inf, F32)
        l_ref[...] = jnp.zeros(l_ref.shape, F32)
        acc_ref[...] = jnp.zeros(acc_ref.shape, F32)
        r = lax.broadcasted_iota(jnp.int32, bias_ref.shape, 0)
        c = lax.broadcasted_iota(jnp.int32, bias_ref.shape, 1)
        bias_ref[...] = jnp.where((c & (grp - 1)) == (r >> tshift), 0.0, -jnp.inf)

    def swap_halves(v):
        n = v.shape[0]
        return pltpu.roll(v.reshape(n // grp, grp, dh), heads, axis=1).reshape(n, dh)

    qs = (q_ref[...] * scale).astype(BF16)

    def partial_softmax(s, vcat):
        m_g = jnp.max(s, axis=-1, keepdims=True)
        p = jnp.exp(s - m_g)
        return m_g, jnp.sum(p, axis=-1, keepdims=True), _dot(p.astype(BF16), vcat)

    def merge(parts):
        m_prev = m_ref[...]
        m_new = m_prev
        for m_g, _, _ in parts:
            m_new = jnp.maximum(m_new, m_g)
        alpha = jnp.exp(m_prev - m_new)
        l_new = alpha * l_ref[...]
        acc_new = alpha * acc_ref[...]
        for m_g, l_g, acc_g in parts:
            w_g = jnp.exp(m_g - m_new)
            l_new = l_new + w_g * l_g
            acc_new = acc_new + w_g * acc_g
        l_ref[...] = l_new
        acc_ref[...] = acc_new
        m_ref[...] = m_new

    ppg = npg // DEC_KEY_GROUPS
    bias = bias_ref[...]
    parts = []
    for g in range(DEC_KEY_GROUPS):
        for pg in range(g * ppg, (g + 1) * ppg):
            rows = slice(pg * pr, (pg + 1) * pr)
            a_ref[rows, :] = k_pages[pg][...].astype(BF16)
            vpg = v_pages[pg][...]
            b_ref[rows, 0:dh] = vpg.astype(BF16)
            b_ref[rows, dh:2 * dh] = swap_halves(vpg).astype(BF16)
        rows = slice(g * ppg * pr, (g + 1) * ppg * pr)
        parts.append(partial_softmax(_dot_nt(qs, a_ref[rows, :]) + bias, b_ref[rows, :]))
    merge(parts)

    @pl.when(s_id == n_steps - 1)
    def _():
        nk = kn_ref.shape[0]
        r = lax.broadcasted_iota(jnp.int32, (nq, nk), 0)
        c = lax.broadcasted_iota(jnp.int32, (nq, nk), 1)
        ok = ((c & (grp - 1)) == (r >> tshift)) & ((c >> (grp.bit_length() - 1)) <= (r & (tdec - 1)))
        s_new = jnp.where(ok, _dot_nt(qs, kn_ref[...].astype(BF16)), -jnp.inf)
        vn = vn_ref[...]
        merge([partial_softmax(s_new, jnp.concatenate([vn, swap_halves(vn)], axis=1).astype(BF16))])
        o = acc_ref[...] / l_ref[...]
        half = nq // 2
        o1 = o[:half]
        o2 = jnp.concatenate([o[half:, dh:], o[half:, :dh]], axis=1)
        lam = _diff_lambda(lam_ref, lam_init)
        o_ref[...] = _diff_finish(o1, o2, lam, sub_ref[...], lam_init).astype(o_ref.dtype)


def _slab_rows(x, batch, tdec, heads):
    dh = HEAD_DIM
    return x.reshape(batch, tdec, heads, 2, dh).transpose(0, 1, 3, 2, 4).reshape(batch, tdec * 2 * heads, dh)


def _dec_attn_call(page_table, lam_vecs, sub, q_rot, k_new, v_new, cache_k, cache_v, layer, tdec, lam_init):
    batch, n_pages = page_table.shape
    n_pool, depth, page, heads, w = cache_k.shape
    dh = HEAD_DIM
    grp = 2 * heads
    assert tdec & (tdec - 1) == 0 and n_pages % PAGES_PER_STEP == 0 and grp == SUBLANES and w == 2 * dh
    n_steps = n_pages // PAGES_PER_STEP
    pr = page * grp
    nq = grp * tdec
    nk_pad = LANES
    kern = functools.partial(_dec_attn_kernel, n_steps=n_steps, heads=heads, tdec=tdec, lam_init=lam_init)

    def slab_view(cache):
        return cache.reshape(n_pool, depth, page, heads, 2, dh).transpose(0, 1, 2, 4, 3, 5).reshape(n_pool, depth, pr, dh)

    q_slab = q_rot.reshape(batch, tdec, heads, 2, dh).transpose(0, 3, 2, 1, 4).reshape(batch, nq, dh)
    pad = ((0, 0), (0, nk_pad - nq), (0, 0))
    kn_slab = jnp.pad(_slab_rows(k_new, batch, tdec, heads), pad)
    vn_slab = jnp.pad(_slab_rows(v_new, batch, tdec, heads), pad)

    def page_spec(pg):
        return pl.BlockSpec((None, None, pr, dh), lambda b, s, pt: (pt[b, s * PAGES_PER_STEP + pg], layer, 0, 0))

    grid_spec = pltpu.PrefetchScalarGridSpec(
        num_scalar_prefetch=1,
        grid=(batch, n_steps),
        in_specs=[pl.BlockSpec((4, dh), lambda b, s, pt: (0, 0)),
                  pl.BlockSpec((1, w), lambda b, s, pt: (0, 0)),
                  pl.BlockSpec((None, nq, dh), lambda b, s, pt: (b, 0, 0)),
                  pl.BlockSpec((None, nk_pad, dh), lambda b, s, pt: (b, 0, 0)),
                  pl.BlockSpec((None, nk_pad, dh), lambda b, s, pt: (b, 0, 0))]
                 + [page_spec(pg) for pg in range(PAGES_PER_STEP)] * 2,
        out_specs=pl.BlockSpec((None, nq // 2, w), lambda b, s, pt: (b, 0, 0)),
        scratch_shapes=[pltpu.VMEM((PAGES_PER_STEP * pr, dh), BF16), pltpu.VMEM((PAGES_PER_STEP * pr, w), BF16),
                        pltpu.VMEM((nq, PAGES_PER_STEP // DEC_KEY_GROUPS * pr), F32),
                        pltpu.VMEM((nq, 1), F32), pltpu.VMEM((nq, 1), F32), pltpu.VMEM((nq, w), F32)],
    )
    o = pl.pallas_call(
        kern,
        out_shape=SDS((batch, nq // 2, w), BF16),
        grid_spec=grid_spec,
        compiler_params=_cp(("arbitrary", "arbitrary")),
        name="diff_attn_decode",
    )(page_table, lam_vecs, sub, q_slab, kn_slab, vn_slab,
      *([slab_view(cache_k)] * PAGES_PER_STEP), *([slab_view(cache_v)] * PAGES_PER_STEP))
    return o.reshape(batch, heads, tdec, w).transpose(0, 2, 1, 3).reshape(batch * tdec, heads * w)


def _causal_conv(xp_ref, x, prev, buf0, is_first, w, tin):
    @pl.when(is_first)
    def _():
        xp_ref[0:SUBLANES, :] = buf0

    @pl.when(jnp.logical_not(is_first))
    def _():
        xp_ref[0:SUBLANES, :] = prev

    xp_ref[SUBLANES:SUBLANES + tin, :] = x
    base = SUBLANES - (CONV_W - 1)
    y = xp_ref[pl.ds(base, tin), :] * w[0:1]
    for j in range(1, CONV_W):
        y = y + xp_ref[pl.ds(base + j, tin), :] * w[j:j + 1]
    return y


def _l2norm(x):
    return x * lax.rsqrt(jnp.sum(x * x, axis=-1, keepdims=True) + EPS)


def _bmm(a, b):
    return jnp.einsum('bij,bjk->bik', a, b, preferred_element_type=F32)


def _bmm_nt(a, b):
    return jnp.einsum('bik,bjk->bij', a, b, preferred_element_type=F32)


def _gdn_chunk_math(y, g, beta, u_ref, w_ref, qt_ref, kt_ref, qk_ref, eg_ref, *, heads, n_levels):
    nb, c_len, _ = y.shape
    dh = HEAD_DIM
    gw = heads * dh
    r = lax.broadcasted_iota(jnp.int32, (c_len, c_len), 0)
    c = lax.broadcasted_iota(jnp.int32, (c_len, c_len), 1)
    incl = r >= c
    strict = r > c
    tri = incl.astype(F32)
    gcs = [jnp.dot(tri, g[b], precision=lax.Precision.HIGHEST, preferred_element_type=F32) for b in range(nb)]
    gc = jnp.stack(gcs)
    gc_t = jnp.stack([x.T for x in gcs])
    level_masks = [((r >> sh) == (c >> sh) + 1) & ((r >> (sh + 1)) == (c >> (sh + 1))) for sh in range(n_levels)]

    for h in range(heads):
        qh = _l2norm(y[:, :, h * dh:(h + 1) * dh]) * (dh ** -0.5)
        kh = _l2norm(y[:, :, gw + h * dh:gw + (h + 1) * dh])
        vh = y[:, :, 2 * gw + h * dh:2 * gw + (h + 1) * dh]
        gcol = gc[:, :, heads + h:heads + h + 1]
        grow = gc_t[:, heads + h:heads + h + 1, :]
        bcol = beta[:, :, h:h + 1]
        glast = gc[:, c_len - 1:c_len, heads + h:heads + h + 1]
        dmat = jnp.where(incl, jnp.exp(jnp.minimum(gcol - grow, 0.0)), 0.0)
        kb = kh * bcol
        kh16 = kh.astype(BF16)
        lmat = jnp.where(strict, _bmm_nt(kb.astype(BF16), kh16) * dmat, 0.0)

        ymat = None
        for sh in range(n_levels):
            e = jnp.where(level_masks[sh], lmat, 0.0)
            if ymat is None:
                ymat = -e
            else:
                y16 = ymat.astype(BF16)
                f = e + _bmm(y16, e.astype(BF16))
                ymat = ymat - (f + _bmm(f.astype(BF16), y16))

        egc = jnp.exp(gcol)
        rhs = jnp.concatenate([vh * bcol, kb * egc], axis=2)
        sol = rhs + _bmm(ymat.astype(BF16), rhs.astype(BF16))
        lanes = slice(h * dh, (h + 1) * dh)
        u_ref[:, :, lanes] = sol[:, :, :dh]
        w_ref[:, :, lanes] = sol[:, :, dh:]
        qt_ref[:, :, lanes] = qh * egc
        kt_ref[:, :, lanes] = kh * jnp.exp(glast - gcol)
        qk_ref[:, :, lanes] = jnp.where(incl, _bmm_nt(qh.astype(BF16), kh16) * dmat, 0.0)
        eg_ref[:, :, lanes] = jnp.broadcast_to(jnp.exp(glast), (nb, SUBLANES, dh))


def _gdn_gates(blk, avec, dtvec):
    beta = jax.nn.sigmoid(blk)
    g = -jnp.exp(avec) * jax.nn.softplus(blk + dtvec)
    return g, beta


def _gdn_prep_seq_kernel(x_ref, prev_ref, buf0_ref, cw_ref, gba_ref, avec_ref, dtvec_ref, gz_ref,
                         u_ref, w_ref, qt_ref, kt_ref, qk_ref, eg_ref, gzo_ref, xp_ref, *, nb, tiles_per_seq, heads):
    c_len = GDN_CHUNK
    tin = nb * c_len
    gzo_ref[...] = gz_ref[...].reshape(gzo_ref.shape)
    is_first = (pl.program_id(0) % tiles_per_seq) == 0
    y = _silu(_causal_conv(xp_ref, x_ref[...], prev_ref[...], buf0_ref[...], is_first, cw_ref[...], tin))
    g, beta = _gdn_gates(gba_ref[...], avec_ref[...], dtvec_ref[...])
    _gdn_chunk_math(y.reshape(nb, c_len, y.shape[1]), g.reshape(nb, c_len, LANES), beta.reshape(nb, c_len, LANES),
                    u_ref, w_ref, qt_ref, kt_ref, qk_ref, eg_ref, heads=heads, n_levels=c_len.bit_length() - 1)


def _gdn_prep_short_kernel(x_ref, buf0_ref, cw_ref, gba_ref, avec_ref, dtvec_ref, gz_ref,
                           u_ref, w_ref, qt_ref, kt_ref, qk_ref, eg_ref, gzo_ref, xp_ref, ypad_ref, gpad_ref, bpad_ref,
                           *, nb, seq, heads):
    cw = x_ref.shape[1]
    gzo_ref[...] = jnp.zeros(gzo_ref.shape, F32)
    gzo_ref[:, 0:seq, :] = gz_ref[...].reshape(nb, seq, gzo_ref.shape[2])
    xp_ref[:, 0:SUBLANES, :] = buf0_ref[...].reshape(nb, SUBLANES, cw)
    xp_ref[:, SUBLANES:SUBLANES + seq, :] = x_ref[...].reshape(nb, seq, cw)
    base = SUBLANES - (CONV_W - 1)
    w = cw_ref[...]
    y = xp_ref[:, pl.ds(base, seq), :] * w[0:1]
    for j in range(1, CONV_W):
        y = y + xp_ref[:, pl.ds(base + j, seq), :] * w[j:j + 1]
    y = _silu(y)
    g, beta = _gdn_gates(gba_ref[...], avec_ref[...], dtvec_ref[...])
    ypad_ref[...] = jnp.zeros(ypad_ref.shape, F32)
    gpad_ref[...] = jnp.zeros(gpad_ref.shape, F32)
    bpad_ref[...] = jnp.zeros(bpad_ref.shape, F32)
    ypad_ref[:, 0:seq, :] = y
    gpad_ref[:, 0:seq, :] = g.reshape(nb, seq, LANES)
    bpad_ref[:, 0:seq, :] = beta.reshape(nb, seq, LANES)
    _gdn_chunk_math(ypad_ref[...], gpad_ref[...], bpad_ref[...], u_ref, w_ref, qt_ref, kt_ref, qk_ref, eg_ref,
                    heads=heads, n_levels=seq.bit_length() - 1)


def _gdn_prep_call(z, row0, batch, seq, buf0, conv_w, avec, dtvec, heads, nb):
    c_len = GDN_CHUNK
    gw = heads * HEAD_DIM
    cw = 3 * gw
    gba_blk = GBA_COL // LANES
    vec = pl.BlockSpec((1, LANES), lambda i: (0, 0))
    if seq >= c_len:
        tin = nb * c_len
        assert seq % tin == 0 and row0 % tin == 0
        tiles_per_seq = seq // tin
        n_tiles = batch * tiles_per_seq
        blk0 = row0 // tin
        prev0 = row0 // SUBLANES
        kern = functools.partial(_gdn_prep_seq_kernel, nb=nb, tiles_per_seq=tiles_per_seq, heads=heads)
        in_specs = [pl.BlockSpec((tin, cw), lambda i: (blk0 + i, 0)),
                    pl.BlockSpec((SUBLANES, cw), lambda i: (jnp.maximum(prev0 + i * (tin // SUBLANES) - 1, 0), 0)),
                    pl.BlockSpec((SUBLANES, cw), lambda i: (i // tiles_per_seq, 0)),
                    pl.BlockSpec((CONV_W, cw), lambda i: (0, 0)),
                    pl.BlockSpec((tin, LANES), lambda i: (blk0 + i, gba_blk)), vec, vec,
                    pl.BlockSpec((tin, gw), lambda i: (blk0 + i, COL_GZ // gw))]
        scratch = [pltpu.VMEM((tin + SUBLANES, cw), F32)]
        args = (z, z, buf0, conv_w, z, avec, dtvec, z)
    else:
        assert seq == SUBLANES and batch % nb == 0 and row0 % (nb * seq) == 0
        tin = nb * seq
        n_tiles = batch // nb
        blk0 = row0 // tin
        kern = functools.partial(_gdn_prep_short_kernel, nb=nb, seq=seq, heads=heads)
        in_specs = [pl.BlockSpec((tin, cw), lambda i: (blk0 + i, 0)),
                    pl.BlockSpec((nb * SUBLANES, cw), lambda i: (i, 0)),
                    pl.BlockSpec((CONV_W, cw), lambda i: (0, 0)),
                    pl.BlockSpec((tin, LANES), lambda i: (blk0 + i, gba_blk)), vec, vec,
                    pl.BlockSpec((tin, gw), lambda i: (blk0 + i, COL_GZ // gw))]
        scratch = [pltpu.VMEM((nb, SUBLANES + seq, cw), F32), pltpu.VMEM((nb, c_len, cw), F32),
                   pltpu.VMEM((nb, c_len, LANES), F32), pltpu.VMEM((nb, c_len, LANES), F32)]
        args = (z, buf0, conv_w, z, avec, dtvec, z)
    n_chunks = n_tiles * nb
    out_blk = pl.BlockSpec((nb, c_len, gw), lambda i: (i, 0, 0))
    return pl.pallas_call(
        kern,
        out_shape=(tuple(SDS((n_chunks, c_len, gw), F32) for _ in range(5)) + (SDS((n_chunks, SUBLANES, gw), F32),)
                   + (SDS((n_chunks, c_len, gw), F32),)),
        grid=(n_tiles,),
        in_specs=in_specs,
        out_specs=(out_blk,) * 5 + (pl.BlockSpec((nb, SUBLANES, gw), lambda i: (i, 0, 0)), out_blk),
        scratch_shapes=scratch,
        compiler_params=_cp(("arbitrary",)),
        name="gdn_prep",
    )(*args)


def _gdn_scan_kernel(u_ref, w_ref, qt_ref, kt_ref, qk_ref, eg_ref, s0_ref, gz_ref, gn_ref, o_ref, s_out_ref, s_ref,
                     *, n_c, tin, heads, nb):
    c_len = GDN_CHUNK
    dh = HEAD_DIM
    ci = pl.program_id(1)

    @pl.when(ci == 0)
    def _():
        s_ref[...] = s0_ref[...]

    gn = gn_ref[...]
    for h in range(heads):
        lanes = slice(h * dh, (h + 1) * dh)
        s_h = s_ref[:, h]
        s16 = s_h.astype(BF16)
        wq = jnp.concatenate([w_ref[:, :, lanes], qt_ref[:, :, lanes]], axis=1).astype(BF16)
        ws_qs = jnp.einsum('bck,bkv->bcv', wq, s16, preferred_element_type=F32)
        v_new = u_ref[:, :, lanes] - ws_qs[:, :c_len]
        v16 = v_new.astype(BF16)
        o = ws_qs[:, c_len:] + jnp.einsum('bij,bjv->biv', qk_ref[:, :, lanes].astype(BF16), v16,
                                          preferred_element_type=F32)
        eg = eg_ref[:, 0:1, lanes]
        s_ref[:, h] = s_h * eg + jnp.einsum('bck,bcv->bkv', kt_ref[:, :, lanes].astype(BF16), v16,
                                            preferred_element_type=F32)
        o = o[:, :tin]
        o = o * lax.rsqrt(jnp.mean(o * o, axis=-1, keepdims=True) + EPS) * gn
        o_ref[:, :, lanes] = (o * _silu(gz_ref[:, 0:tin, lanes])).astype(o_ref.dtype)

    @pl.when(ci == n_c - 1)
    def _():
        s_out_ref[...] = s_ref[...]


def _gdn_scan_call(prep, s0, gn, batch, seq, heads, nb):
    u, w, qt, kt, qk, eg, gz = prep
    c_len = GDN_CHUNK
    dh = HEAD_DIM
    gw = heads * dh
    tin = min(seq, c_len)
    n_c = max(seq // c_len, 1)
    kern = functools.partial(_gdn_scan_kernel, n_c=n_c, tin=tin, heads=heads, nb=nb)
    blk = pl.BlockSpec((nb, c_len, gw), lambda b, i: (b, i, 0))
    return pl.pallas_call(
        kern,
        out_shape=(SDS((batch, seq, gw), BF16), SDS((batch, heads, dh, dh), F32)),
        grid=(batch // nb, n_c),
        in_specs=[blk, blk, blk, blk, blk,
                  pl.BlockSpec((nb, SUBLANES, gw), lambda b, i: (b, i, 0)),
                  pl.BlockSpec((nb, heads, dh, dh), lambda b, i: (b, 0, 0, 0)),
                  blk,
                  pl.BlockSpec((1, dh), lambda b, i: (0, 0))],
        out_specs=(pl.BlockSpec((nb, tin, gw), lambda b, i: (b, i, 0)),
                   pl.BlockSpec((nb, heads, dh, dh), lambda b, i: (b, 0, 0, 0))),
        scratch_shapes=[pltpu.VMEM((nb, heads, dh, dh), F32)],
        compiler_params=_cp(("arbitrary", "arbitrary")),
        name="gdn_scan",
    )(u, w, qt, kt, qk, eg, s0, gz, gn)


def _lru_kernel(x_ref, prev_ref, buf0_ref, cw_ref, cb_ref, wa_ref, ba_ref, wx_ref, bx_ref, lam_ref, h0_ref, lg_ref,
                o_ref, hl_ref, xp_ref, hc_ref, *, tin):
    i = pl.program_id(1)

    @pl.when(i == 0)
    def _():
        hc_ref[...] = h0_ref[...]

    xc = _causal_conv(xp_ref, x_ref[...], prev_ref[...], buf0_ref[...], i == 0, cw_ref[...], tin) + cb_ref[...]
    x16 = xc.astype(BF16)
    rg = jax.nn.sigmoid(_dot(x16, wa_ref[...]) + ba_ref[...])
    ig = jax.nn.sigmoid(_dot(x16, wx_ref[...]) + bx_ref[...])
    log_a = -LRU_C * rg * jax.nn.softplus(-lam_ref[...])
    a = jnp.exp(log_a)
    t = jnp.tanh(log_a)
    b = jnp.sqrt(-2.0 * t / (1.0 - t)) * (ig * xc)

    row = lax.broadcasted_iota(jnp.int32, a.shape, 0)
    d = 1
    while d < tin:
        keep = row >= d
        a_s = jnp.where(keep, pltpu.roll(a, d, 0), 1.0)
        b_s = jnp.where(keep, pltpu.roll(b, d, 0), 0.0)
        b = a * b_s + b
        a = a * a_s
        d *= 2
    hseq = b + a * hc_ref[...]
    hc_ref[...] = hseq[tin - 1:tin, :]
    hl_ref[...] = hseq[tin - 1:tin, :]
    o_ref[...] = (hseq * jax.nn.gelu(lg_ref[...])).astype(o_ref.dtype)


def _lru_call(z, row0, batch, seq, buf0, conv_w, conv_b, wa, ba, wx, bx, lam, h0, col_x, col_g, tin):
    w = conv_w.shape[1]
    n_t = seq // tin
    assert seq == n_t * tin and row0 % tin == 0 and tin & (tin - 1) == 0
    blk0 = row0 // tin
    prev_per_tile = tin // SUBLANES
    prev0 = row0 // SUBLANES
    kern = functools.partial(_lru_kernel, tin=tin)
    vec = pl.BlockSpec((1, w), lambda b, i: (0, 0))
    return pl.pallas_call(
        kern,
        out_shape=(SDS((batch, seq, w), BF16), SDS((batch, 1, w), F32)),
        grid=(batch, n_t),
        in_specs=[pl.BlockSpec((tin, w), lambda b, i: (blk0 + b * n_t + i, col_x // w)),
                  pl.BlockSpec((SUBLANES, w),
                               lambda b, i: (jnp.maximum(prev0 + (b * n_t + i) * prev_per_tile - 1, 0), col_x // w)),
                  pl.BlockSpec((SUBLANES, w), lambda b, i: (b, 0)),
                  pl.BlockSpec((CONV_W, w), lambda b, i: (0, 0)),
                  vec,
                  pl.BlockSpec((w, w), lambda b, i: (0, 0)), vec,
                  pl.BlockSpec((w, w), lambda b, i: (0, 0)), vec,
                  vec,
                  pl.BlockSpec((None, 1, w), lambda b, i: (b, 0, 0)),
                  pl.BlockSpec((tin, w), lambda b, i: (blk0 + b * n_t + i, col_g // w))],
        out_specs=(pl.BlockSpec((None, tin, w), lambda b, i: (b, i, 0)),
                   pl.BlockSpec((None, 1, w), lambda b, i: (b, 0, 0))),
        scratch_shapes=[pltpu.VMEM((tin + SUBLANES, w), F32), pltpu.VMEM((1, w), F32)],
        compiler_params=_cp(("arbitrary", "arbitrary")),
        name="rglru",
    )(z, z, buf0, conv_w, conv_b, wa, ba, wx, bx, lam, h0, z)


GDN_W = 512
DIFF_W = 1024
LRU_W = 512
COL_GQKV = 0
COL_GZ = 3 * GDN_W
COL_DQ = COL_GZ + GDN_W
COL_DK = COL_DQ + DIFF_W
COL_DV = COL_DK + DIFF_W
COL_LX = COL_DV + DIFF_W
COL_LG = COL_LX + LRU_W
GBA_COL = COL_LG + LRU_W
NZ = 6400


def _pad_state_rows(buf):
    b, k, c = buf.shape
    return jnp.pad(buf, ((0, 0), (SUBLANES - k, 0), (0, 0))).reshape(b * SUBLANES, c)


def kernel(x_prompt, x_sample, cache_k, cache_v, state_gdn, state_gdn_conv, state_lru, state_lru_conv, page_table,
           c_prompt, c_sample, ada_w, ada_b, norm_ffn1, ffn1_w_gate, ffn1_w_up, ffn1_w_down, norm_mix, w_in,
           gdn_conv_w, gdn_a_log, gdn_dt_bias, gdn_norm, diff_lq1, diff_lk1, diff_lq2, diff_lk2, diff_subln,
           lru_conv_w, lru_conv_b, lru_wa, lru_ba, lru_wx, lru_bx, lru_lambda, w_out, norm_ffn2, ffn2_w_gate,
           ffn2_w_up, ffn2_w_down, final_norm):
    bp, tp, d = x_prompt.shape
    bd, td, _ = x_sample.shape
    depth = ada_w.shape[0]
    gdn_heads = gdn_a_log.shape[1]
    diff_heads = cache_k.shape[3]
    page = cache_k.shape[2]
    past_len = page_table.shape[1] * page
    mp_rows = bp * tp
    m = mp_rows + bd * td
    tm = math.gcd(m, ROW_TILE)
    groups_per_seq = tp // SUBLANES
    assert td == SUBLANES and gdn_heads * HEAD_DIM == GDN_W and d == 2048 and groups_per_seq & (groups_per_seq - 1) == 0
    rmap = _RowMap(groups_prompt=mp_rows // SUBLANES, shift_prompt=groups_per_seq.bit_length() - 1, n_prompt=bp)

    x = jnp.concatenate([x_prompt.reshape(mp_rows, d), x_sample.reshape(bd * td, d)], axis=0)
    n_c = bp + bd
    c_all = jnp.pad(jnp.concatenate([c_prompt, c_sample], axis=0), ((0, (-n_c) % LANES), (0, 0)))
    mod = _ada_call(c_all, ada_w, ada_b)

    zeros_gdn_buf = jnp.zeros((bp * SUBLANES, 3 * GDN_W), F32)
    zeros_lru_buf = jnp.zeros((bp * SUBLANES, LRU_W), F32)
    w_out16 = w_out.astype(BF16)
    tail = CONV_W - 1

    def last_rows(zz, row0, bsz, seq, col, width):
        if seq <= SUBLANES:
            return zz[row0:row0 + bsz * seq, col:col + width].reshape(bsz, seq, width)[:, seq - tail:]
        return jnp.stack([zz[row0 + (b + 1) * seq - tail:row0 + (b + 1) * seq, col:col + width] for b in range(bsz)])

    outs = {k: [] for k in ("k_p", "v_p", "s_p", "sb_p", "h_p", "hb_p", "k_s", "v_s", "s_s", "sb_s", "h_s", "hb_s")}
    prompt_kv = None
    for l in range(depth):
        lam_init = 0.8 - 0.6 * math.exp(-0.3 * l)

        x = _ffn_call(x, norm_ffn1[l][None], mod, l, 0, ffn1_w_gate, ffn1_w_up, ffn1_w_down, rmap,
                      tm=tm, tf=512, tf_first=256)

        wl = w_in[l]
        o1 = 4 * GDN_W
        o2 = o1 + 2 * gdn_heads
        w_re = jnp.concatenate([wl[:, :o1], wl[:, o2:], wl[:, o1:o2]], axis=1)
        w_re = jnp.pad(w_re, ((0, 0), (0, NZ - w_re.shape[1]))).astype(BF16)
        z = _proj_in_call(x, norm_mix[l][None], mod, l, 3, w_re, rmap, tm=tm, tn=1280)

        lam_vecs = jnp.stack([diff_lq1[l], diff_lk1[l], diff_lq2[l], diff_lk2[l]])
        sub = diff_subln[l][None]
        avec = jnp.zeros((1, LANES), F32).at[0, gdn_heads:2 * gdn_heads].set(gdn_a_log[l])
        dtvec = jnp.zeros((1, LANES), F32).at[0, gdn_heads:2 * gdn_heads].set(gdn_dt_bias[l])
        bw = lru_wa.shape[2]
        eye = jnp.eye(LRU_BLOCKS, dtype=F32)
        wa_full = (eye[:, None, :, None] * lru_wa[l][:, :, None, :]).reshape(LRU_W, LRU_W).astype(BF16)
        wx_full = (eye[:, None, :, None] * lru_wx[l][:, :, None, :]).reshape(LRU_W, LRU_W).astype(BF16)

        mixes = {}
        for path in ("p", "s"):
            if path == "p":
                row0, bsz, seq, pos0 = 0, bp, tp, 0
                gdn_buf0, lru_buf0 = zeros_gdn_buf, zeros_lru_buf
                s0 = jnp.zeros((bp, gdn_heads, HEAD_DIM, HEAD_DIM), F32)
                h0 = jnp.zeros((bp, 1, LRU_W), F32)
            else:
                row0, bsz, seq, pos0 = mp_rows, bd, td, past_len
                gdn_buf0 = _pad_state_rows(state_gdn_conv[:, l])
                lru_buf0 = _pad_state_rows(state_lru_conv[:, l])
                s0 = state_gdn[:, l]
                h0 = state_lru[:, l][:, None, :]
            rows = bsz * seq

            prep = _gdn_prep_call(z, row0, bsz, seq, gdn_buf0, gdn_conv_w[l], avec, dtvec, gdn_heads,
                                  nb=math.gcd(4, seq // GDN_CHUNK) if path == "p" else math.gcd(bsz, 8))
            prep3 = tuple(p.reshape(bsz, -1, GDN_W) for p in prep)
            o_gdn, s_new = _gdn_scan_call(prep3, s0, gdn_norm[l][None], bsz, seq, gdn_heads, nb=math.gcd(bsz, 4))

            if path == "p":
                q_rot, k_rot, *prompt_kv = _rope_call(z, row0, rows, seq, pos0, COL_DQ, COL_DK, COL_DV, DIFF_W,
                                                      tq=math.gcd(seq, 256), slab=(l, depth, diff_heads, prompt_kv))
                o_diff = _flash_call(lam_vecs, sub, q_rot, k_rot, z, COL_DV, bsz, seq, diff_heads, lam_init,
                                     tq=math.gcd(seq, 512))
            else:
                q_rot, k_rot, v_rows = _rope_call(z, row0, rows, seq, pos0, COL_DQ, COL_DK, COL_DV, DIFF_W,
                                                  tq=math.gcd(rows, 256))
                o_diff = _dec_attn_call(page_table, lam_vecs, sub, q_rot, k_rot, v_rows, cache_k, cache_v, l, seq,
                                        lam_init)
                outs["k_s"].append(k_rot.reshape(bsz, seq, diff_heads, 2 * HEAD_DIM))
                outs["v_s"].append(v_rows.reshape(bsz, seq, diff_heads, 2 * HEAD_DIM))

            o_lru, h_last = _lru_call(z, row0, bsz, seq, lru_buf0, lru_conv_w[l], lru_conv_b[l][None], wa_full,
                                      lru_ba[l][None], wx_full, lru_bx[l][None], lru_lambda[l][None], h0,
                                      COL_LX, COL_LG, tin=min(seq, 256))

            mixes[path] = (o_gdn.reshape(rows, GDN_W), o_diff.reshape(rows, DIFF_W), o_lru.reshape(rows, LRU_W))
            outs["s_" + path].append(s_new)
            outs["sb_" + path].append(last_rows(z, row0, bsz, seq, COL_GQKV, 3 * GDN_W))
            outs["h_" + path].append(h_last.reshape(bsz, LRU_W))
            outs["hb_" + path].append(last_rows(z, row0, bsz, seq, COL_LX, LRU_W))

        x = _proj_out_call(mixes["p"], mixes["s"], w_out16, l, x, mod, 5, rmap,
                           tm=math.gcd(math.gcd(mp_rows, bd * td), 256))
        x = _ffn_call(x, norm_ffn2[l][None], mod, l, 6, ffn2_w_gate, ffn2_w_up, ffn2_w_down, rmap,
                      tm=tm, tf=512, tf_first=256)

    y_prompt = _final_norm_call(x, final_norm[None], 0, mp_rows, math.gcd(mp_rows, 256)).reshape(bp, tp, d)
    y_sample = _final_norm_call(x, final_norm[None], mp_rows, bd * td, math.gcd(bd * td, 256)).reshape(bd, td, d)
    st = {k: jnp.stack(v, axis=1) for k, v in outs.items() if v}

    def from_slab(s):
        return (s.reshape(bp, depth, tp, 2, diff_heads, HEAD_DIM).transpose(0, 1, 2, 4, 3, 5)
                .reshape(bp, depth, tp, diff_heads, 2 * HEAD_DIM))

    st["k_p"], st["v_p"] = (from_slab(s) for s in prompt_kv)
    return (y_prompt, y_sample, st["k_p"], st["v_p"], st["s_p"], st["sb_p"], st["h_p"], st["hb_p"],
            st["k_s"], st["v_s"], st["s_s"], st["sb_s"], st["h_s"], st["hb_s"])
```

```python
import functools
import math
from typing import NamedTuple

import jax
import jax.numpy as jnp
from jax import lax
from jax.experimental import pallas as pl
from jax.experimental.pallas import tpu as pltpu

F32 = jnp.float32
BF16 = jnp.bfloat16
SDS = jax.ShapeDtypeStruct

EPS = 1e-6
HEAD_DIM = 128
CONV_W = 4
ROPE_THETA = 10000.0
LRU_C = 8.0
N_MOD = 9
LRU_BLOCKS = 8
SUBLANES = 8
LANES = 128
VMEM_LIMIT = 60 * 1024 * 1024
GDN_CHUNK = 128
PAGES_PER_STEP = 16
ROW_TILE = 768
DEC_KEY_GROUPS = 4


def _cp(sem):
    return pltpu.CompilerParams(dimension_semantics=sem, vmem_limit_bytes=VMEM_LIMIT)


def _dot(a, b):
    return jnp.dot(a, b, preferred_element_type=F32)


def _dot_nt(a, b):
    return lax.dot_general(a, b, (((1,), (1,)), ((), ())), preferred_element_type=F32)


def _silu(x):
    return x * jax.nn.sigmoid(x)


class _RowMap(NamedTuple):
    groups_prompt: int
    shift_prompt: int
    n_prompt: int


def _group_mods(mod_ref, tile, t8, rmap):
    mp = mod_ref.shape[0]
    g = tile * t8 + lax.broadcasted_iota(jnp.int32, (t8, mp), 0)
    col = lax.broadcasted_iota(jnp.int32, (t8, mp), 1)
    src = jnp.where(g < rmap.groups_prompt, g >> rmap.shift_prompt, g - rmap.groups_prompt + rmap.n_prompt)
    return jnp.dot((col == src).astype(F32), mod_ref[...], precision=lax.Precision.HIGHEST, preferred_element_type=F32)


def _norm_mod(x, gain, shift8, scale8):
    tm, d = x.shape
    y = x * lax.rsqrt(jnp.mean(x * x, axis=-1, keepdims=True) + EPS) * gain
    y3 = y.reshape(tm // SUBLANES, SUBLANES, d)
    y3 = y3 * (1.0 + scale8[:, None, :]) + shift8[:, None, :]
    return y3.reshape(tm, d)


def _gated_residual(x, y, gate8, half):
    tm, d = x.shape
    g = gate8 * 0.5 if half else gate8
    y3 = y.reshape(tm // SUBLANES, SUBLANES, d) * g[:, None, :]
    return x + y3.reshape(tm, d)


def _ada_kernel(c_ref, w_ref, b_ref, o_ref):
    c = c_ref[...]
    o_ref[0] = _dot(_silu(c).astype(BF16), w_ref[0].astype(BF16)) + b_ref[0]


def _ada_call(c_all, ada_w, ada_b):
    n_layers, d, n = ada_w.shape
    mp = c_all.shape[0]
    tn = 1024
    return pl.pallas_call(
        _ada_kernel,
        out_shape=SDS((n_layers, mp, n), F32),
        grid=(n_layers, n // tn),
        in_specs=[pl.BlockSpec((mp, d), lambda l, j: (0, 0)),
                  pl.BlockSpec((1, d, tn), lambda l, j: (l, 0, j)),
                  pl.BlockSpec((1, 1, tn), lambda l, j: (l, 0, j))],
        out_specs=pl.BlockSpec((1, mp, tn), lambda l, j: (l, 0, j)),
        compiler_params=_cp(("arbitrary", "arbitrary")),
        name="ada_proj",
    )(c_all, ada_w, ada_b.reshape(n_layers, 1, n))


def _ffn_kernel(x_ref, gain_ref, sh_ref, sc_ref, gt_ref, wg_ref, wu_ref, wd_ref, *rest, n_j, col_chunk, rmap, tile0,
                emit_weights):
    if emit_weights:
        o_ref, wgo_ref, wuo_ref, wdo_ref, h_ref = rest
    else:
        o_ref, h_ref = rest[-2:]
    i = pl.program_id(0) + tile0
    j = pl.program_id(1)
    t8 = x_ref.shape[0] // SUBLANES

    @pl.when(j == 0)
    def _():
        sh8 = _group_mods(sh_ref, i, t8, rmap)
        sc8 = _group_mods(sc_ref, i, t8, rmap)
        h_ref[...] = _norm_mod(x_ref[...], gain_ref[...], sh8, sc8).astype(BF16)
        o_ref[...] = jnp.zeros_like(o_ref)

    h = h_ref[...]
    wg, wu, wd = wg_ref[...], wu_ref[...], wd_ref
    if emit_weights:
        wg, wu, wd = wg.astype(BF16), wu.astype(BF16), wd[...].astype(BF16)
        wgo_ref[...] = wg
        wuo_ref[...] = wu
        wdo_ref[...] = wd
    g = _dot(h, wg)
    u = _dot(h, wu)
    a = (_silu(g) * u).astype(BF16)
    d = o_ref.shape[1]
    for c in range(0, d, col_chunk):
        o_ref[:, c:c + col_chunk] += _dot(a, wd[:, c:c + col_chunk])

    @pl.when(j == n_j - 1)
    def _():
        o_ref[...] = _gated_residual(x_ref[...], o_ref[...], _group_mods(gt_ref, i, t8, rmap), half=True)


def _mod_spec(mod, layer, k):
    mp = mod.shape[1]
    d = mod.shape[2] // N_MOD
    return pl.BlockSpec((None, mp, d), lambda i, j: (layer, 0, k))


def _ffn_call(x, gain, mod, layer, k_shift, wg, wu, wd, rmap, *, tm, tf, tf_first):
    m, d = x.shape
    f = wg.shape[2]
    mods = [_mod_spec(mod, layer, k_shift + k) for k in range(3)]
    common = dict(col_chunk=512, rmap=rmap)
    first = pl.pallas_call(
        functools.partial(_ffn_kernel, n_j=f // tf_first, tile0=0, emit_weights=True, **common),
        out_shape=(SDS((m, d), F32), SDS((d, f), BF16), SDS((d, f), BF16), SDS((f, d), BF16)),
        grid=(1, f // tf_first),
        in_specs=[pl.BlockSpec((tm, d), lambda i, j: (0, 0)),
                  pl.BlockSpec((1, d), lambda i, j: (0, 0))] + mods
                 + [pl.BlockSpec((None, d, tf_first), lambda i, j: (layer, 0, j)),
                    pl.BlockSpec((None, d, tf_first), lambda i, j: (layer, 0, j)),
                    pl.BlockSpec((None, tf_first, d), lambda i, j: (layer, j, 0))],
        out_specs=(pl.BlockSpec((tm, d), lambda i, j: (0, 0)),
                   pl.BlockSpec((d, tf_first), lambda i, j: (0, j)),
                   pl.BlockSpec((d, tf_first), lambda i, j: (0, j)),
                   pl.BlockSpec((tf_first, d), lambda i, j: (j, 0))),
        scratch_shapes=[pltpu.VMEM((tm, d), BF16)],
        compiler_params=_cp(("arbitrary", "arbitrary")),
        name="ffn_swiglu_first",
    )
    out0, wg16, wu16, wd16 = first(x, gain, mod, mod, mod, wg, wu, wd)
    if m == tm:
        return out0
    rest = pl.pallas_call(
        functools.partial(_ffn_kernel, n_j=f // tf, tile0=1, emit_weights=False, **common),
        out_shape=SDS((m, d), F32),
        grid=(m // tm - 1, f // tf),
        in_specs=[pl.BlockSpec((tm, d), lambda i, j: (i + 1, 0)),
                  pl.BlockSpec((1, d), lambda i, j: (0, 0))] + mods
                 + [pl.BlockSpec((d, tf), lambda i, j: (0, j)),
                    pl.BlockSpec((d, tf), lambda i, j: (0, j)),
                    pl.BlockSpec((tf, d), lambda i, j: (j, 0)),
                    pl.BlockSpec(memory_space=pl.ANY)],
        out_specs=pl.BlockSpec((tm, d), lambda i, j: (i + 1, 0)),
        scratch_shapes=[pltpu.VMEM((tm, d), BF16)],
        input_output_aliases={8: 0},
        compiler_params=_cp(("arbitrary", "arbitrary")),
        name="ffn_swiglu",
    )
    return rest(x, gain, mod, mod, mod, wg16, wu16, wd16, out0)


def _proj_in_kernel(x_ref, gain_ref, sh_ref, sc_ref, w_ref, o_ref, h_ref, *, rmap):
    @pl.when(pl.program_id(1) == 0)
    def _():
        i = pl.program_id(0)
        t8 = x_ref.shape[0] // SUBLANES
        sh8 = _group_mods(sh_ref, i, t8, rmap)
        sc8 = _group_mods(sc_ref, i, t8, rmap)
        h_ref[...] = _norm_mod(x_ref[...], gain_ref[...], sh8, sc8).astype(BF16)

    o_ref[...] = _dot(h_ref[...], w_ref[...])


def _proj_in_call(x, gain, mod, layer, k_shift, w, rmap, *, tm, tn):
    m, d = x.shape
    n = w.shape[1]
    return pl.pallas_call(
        functools.partial(_proj_in_kernel, rmap=rmap),
        out_shape=SDS((m, n), F32),
        grid=(m // tm, n // tn),
        in_specs=[pl.BlockSpec((tm, d), lambda i, j: (i, 0)),
                  pl.BlockSpec((1, d), lambda i, j: (0, 0)),
                  _mod_spec(mod, layer, k_shift), _mod_spec(mod, layer, k_shift + 1),
                  pl.BlockSpec((d, tn), lambda i, j: (0, j))],
        out_specs=pl.BlockSpec((tm, tn), lambda i, j: (i, j)),
        scratch_shapes=[pltpu.VMEM((tm, d), BF16)],
        compiler_params=_cp(("arbitrary", "arbitrary")),
        name="proj_in",
    )(x, gain, mod, mod, w)


def _proj_out_kernel(*refs, n_mix, n_prompt_tiles, rmap):
    prompt = refs[:n_mix]
    decode = refs[n_mix:2 * n_mix]
    w_ref, x_ref, gt_ref, o_ref, w16_ref = refs[2 * n_mix:]
    i = pl.program_id(0)
    t8 = x_ref.shape[0] // SUBLANES

    @pl.when(i == 0)
    def _():
        w16_ref[...] = w_ref[...].astype(BF16)

    def run(parts):
        k0 = 0
        y = None
        for a_ref in parts:
            kw = a_ref.shape[1]
            part = _dot(a_ref[...], w16_ref[k0:k0 + kw, :])
            y = part if y is None else y + part
            k0 += kw
        o_ref[...] = _gated_residual(x_ref[...], y, _group_mods(gt_ref, i, t8, rmap), half=False)

    @pl.when(i < n_prompt_tiles)
    def _():
        run(prompt)

    @pl.when(i >= n_prompt_tiles)
    def _():
        run(decode)


def _proj_out_call(prompt_parts, decode_parts, w, layer, x, mod, k_gate, rmap, *, tm):
    m, d = x.shape
    rows_p = prompt_parts[0].shape[0]
    assert rows_p % tm == 0 and (m - rows_p) % tm == 0
    npt = rows_p // tm
    n_mix = len(prompt_parts)
    kern = functools.partial(_proj_out_kernel, n_mix=n_mix, n_prompt_tiles=npt, rmap=rmap)
    p_specs = [pl.BlockSpec((tm, a.shape[1]), lambda i, j: (jnp.minimum(i, npt - 1), 0)) for a in prompt_parts]
    s_specs = [pl.BlockSpec((tm, a.shape[1]), lambda i, j: (jnp.maximum(i - npt, 0), 0)) for a in decode_parts]
    return pl.pallas_call(
        kern,
        out_shape=SDS((m, d), F32),
        grid=(m // tm, 1),
        in_specs=p_specs + s_specs + [pl.BlockSpec((None, w.shape[1], d), lambda i, j: (layer, 0, 0)),
                                      pl.BlockSpec((tm, d), lambda i, j: (i, 0)),
                                      _mod_spec(mod, layer, k_gate)],
        out_specs=pl.BlockSpec((tm, d), lambda i, j: (i, 0)),
        scratch_shapes=[pltpu.VMEM((w.shape[1], d), BF16)],
        compiler_params=_cp(("arbitrary", "arbitrary")),
        name="proj_out",
    )(*prompt_parts, *decode_parts, w, x, mod)


def _final_norm_kernel(x_ref, g_ref, o_ref):
    x = x_ref[...]
    o_ref[...] = x * lax.rsqrt(jnp.mean(x * x, axis=-1, keepdims=True) + EPS) * g_ref[...]


def _final_norm_call(x, gain, row0, rows, tm):
    d = x.shape[1]
    blk0 = row0 // tm
    return pl.pallas_call(
        _final_norm_kernel,
        out_shape=SDS((rows, d), F32),
        grid=(rows // tm,),
        in_specs=[pl.BlockSpec((tm, d), lambda i: (blk0 + i, 0)),
                  pl.BlockSpec((1, d), lambda i: (0, 0))],
        out_specs=pl.BlockSpec((tm, d), lambda i: (i, 0)),
        compiler_params=_cp(("arbitrary",)),
        name="final_norm",
    )(x, gain)


def _rope_kernel(q_ref, k_ref, v_ref, *rest, tq, seq, pos0, slab_heads):
    i = pl.program_id(0)
    if slab_heads:
        qo_ref, ko_ref, kso_ref, vso_ref = rest[-4:]
    else:
        qo_ref, ko_ref, vo_ref = rest
        vo_ref[...] = v_ref[...]
    half = HEAD_DIM // 2
    row = i * tq + lax.broadcasted_iota(jnp.int32, (tq, HEAD_DIM), 0)
    lane = lax.broadcasted_iota(jnp.int32, (tq, HEAD_DIM), 1)
    pos = (pos0 + (row & (seq - 1))).astype(F32)
    freq = (lane & (half - 1)).astype(F32)
    inv = jnp.float32(ROPE_THETA) ** (-freq / half)
    ang = pos * inv
    cos = jnp.cos(ang)
    sin = jnp.where(lane < half, -jnp.sin(ang), jnp.sin(ang))
    grp = 2 * slab_heads

    def slab_rows(g):
        return pl.ds((g % 2) * slab_heads + g // 2, tq, stride=grp)

    for g in range(q_ref.shape[1] // HEAD_DIM):
        lanes = slice(g * HEAD_DIM, (g + 1) * HEAD_DIM)
        x = q_ref[:, lanes]
        qo_ref[:, lanes] = x * cos + pltpu.roll(x, half, 1) * sin
        x = k_ref[:, lanes]
        y = x * cos + pltpu.roll(x, half, 1) * sin
        ko_ref[:, lanes] = y
        if slab_heads:
            kso_ref[slab_rows(g), :] = y
            vso_ref[slab_rows(g), :] = v_ref[:, lanes]


def _rope_call(z, row0, rows, seq, pos0, col_q, col_k, col_v, width, tq, slab=None):
    assert seq & (seq - 1) == 0 and rows % tq == 0 and row0 % tq == 0
    blk0 = row0 // tq
    in_specs = [pl.BlockSpec((tq, width), lambda i: (blk0 + i, col_q // width)),
                pl.BlockSpec((tq, width), lambda i: (blk0 + i, col_k // width)),
                pl.BlockSpec((tq, width), lambda i: (blk0 + i, col_v // width))]
    out_blk = pl.BlockSpec((tq, width), lambda i: (i, 0))
    args = [z, z, z]
    aliases = {}
    if slab is None:
        slab_heads = 0
        out_shape = (SDS((rows, width), F32),) * 3
        out_specs = (out_blk,) * 3
    else:
        layer, depth, slab_heads, prev = slab
        grp = 2 * slab_heads
        tiles_per_seq = seq // tq
        assert seq % tq == 0 and width == grp * HEAD_DIM
        slab_sds = SDS((rows // seq, depth, seq * grp, HEAD_DIM), F32)
        slab_blk = pl.BlockSpec((None, None, tq * grp, HEAD_DIM),
                                lambda i: (i // tiles_per_seq, layer, i % tiles_per_seq, 0))
        out_shape = (SDS((rows, width), F32),) * 2 + (slab_sds,) * 2
        out_specs = (out_blk,) * 2 + (slab_blk,) * 2
        if prev is not None:
            in_specs += [pl.BlockSpec(memory_space=pl.ANY)] * 2
            args += list(prev)
            aliases = {3: 2, 4: 3}
    kern = functools.partial(_rope_kernel, tq=tq, seq=seq, pos0=pos0, slab_heads=slab_heads)
    return pl.pallas_call(
        kern,
        out_shape=out_shape,
        grid=(rows // tq,),
        in_specs=in_specs,
        out_specs=out_specs,
        input_output_aliases=aliases,
        compiler_params=_cp(("arbitrary",)),
        name="rope_qk",
    )(*args)


def _diff_lambda(lam_ref, lam_init):
    v = lam_ref[...]
    s1 = jnp.sum(v[0:1] * v[1:2], axis=-1, keepdims=True)
    s2 = jnp.sum(v[2:3] * v[3:4], axis=-1, keepdims=True)
    return jnp.exp(s1) - jnp.exp(s2) + lam_init


def _diff_finish(o1, o2, lam, sub, lam_init):
    o = o1 - lam * o2
    o = o * lax.rsqrt(jnp.mean(o * o, axis=-1, keepdims=True) + EPS) * sub
    return o * (1.0 - lam_init)


def _flash_kernel(lam_ref, sub_ref, q_ref, k_ref, v_ref, o_ref, kb_ref, vb_ref, m_ref, l_ref, acc_ref,
                  *, tq, lam_init):
    i = pl.program_id(2)
    dh = HEAD_DIM
    scale = dh ** -0.5

    @pl.when(i == 0)
    def _():
        kb_ref[...] = k_ref[...].astype(BF16)
        vb_ref[...] = v_ref[...].astype(BF16)

    q = q_ref[...] * scale
    qs = (q[:, :dh].astype(BF16), q[:, dh:].astype(BF16))
    m_ref[...] = jnp.full(m_ref.shape, -jnp.inf, F32)
    l_ref[...] = jnp.zeros(l_ref.shape, F32)
    acc_ref[...] = jnp.zeros(acc_ref.shape, F32)

    def block(j, masked):
        start = pl.multiple_of(j * tq, tq)
        kblk = kb_ref[pl.ds(start, tq), :]
        vblk = vb_ref[pl.ds(start, tq), :]
        for c in range(2):
            s = _dot_nt(qs[c], kblk[:, c * dh:(c + 1) * dh])
            if masked:
                r = lax.broadcasted_iota(jnp.int32, s.shape, 0)
                cc = lax.broadcasted_iota(jnp.int32, s.shape, 1)
                s = jnp.where(cc <= r, s, -jnp.inf)
            m_prev = m_ref[c]
            m_new = jnp.maximum(m_prev, jnp.max(s, axis=-1, keepdims=True))
            alpha = jnp.exp(m_prev - m_new)
            p = jnp.exp(s - m_new)
            l_ref[c] = alpha * l_ref[c] + jnp.sum(p, axis=-1, keepdims=True)
            acc_ref[c] = alpha * acc_ref[c] + _dot(p.astype(BF16), vblk)
            m_ref[c] = m_new

    def body(j, carry):
        block(j, False)
        return carry

    lax.fori_loop(0, i, body, 0)
    block(i, True)

    lam = _diff_lambda(lam_ref, lam_init)
    o1 = acc_ref[0] / l_ref[0]
    o2 = acc_ref[1] / l_ref[1]
    o_ref[...] = _diff_finish(o1, o2, lam, sub_ref[...], lam_init).astype(o_ref.dtype)


def _flash_call(lam_vecs, sub, q_rot, k_rot, z, col_v, batch, seq, heads, lam_init, tq):
    w = 2 * HEAD_DIM
    nq = seq // tq
    kern = functools.partial(_flash_kernel, tq=tq, lam_init=lam_init)
    return pl.pallas_call(
        kern,
        out_shape=SDS((batch * seq, heads * w), BF16),
        grid=(batch, heads, nq),
        in_specs=[pl.BlockSpec((4, HEAD_DIM), lambda b, h, i: (0, 0)),
                  pl.BlockSpec((1, w), lambda b, h, i: (0, 0)),
                  pl.BlockSpec((tq, w), lambda b, h, i: (b * nq + i, h)),
                  pl.BlockSpec((seq, w), lambda b, h, i: (b, h)),
                  pl.BlockSpec((seq, w), lambda b, h, i: (b, col_v // w + h))],
        out_specs=pl.BlockSpec((tq, w), lambda b, h, i: (b * nq + i, h)),
        scratch_shapes=[pltpu.VMEM((seq, w), BF16), pltpu.VMEM((seq, w), BF16),
                        pltpu.VMEM((2, tq, 1), F32), pltpu.VMEM((2, tq, 1), F32), pltpu.VMEM((2, tq, w), F32)],
        compiler_params=_cp(("arbitrary", "arbitrary", "arbitrary")),
        name="diff_flash_prompt",
    )(lam_vecs, sub, q_rot, k_rot, z)


def _dec_attn_kernel(pt_ref, lam_ref, sub_ref, q_ref, kn_ref, vn_ref, *rest, n_steps, heads, tdec, lam_init):
    npg = PAGES_PER_STEP
    k_pages = rest[:npg]
    v_pages = rest[npg:2 * npg]
    o_ref, a_ref, b_ref, bias_ref, m_ref, l_ref, acc_ref = rest[2 * npg:]
    s_id = pl.program_id(1)
    dh = HEAD_DIM
    scale = dh ** -0.5
    pr = k_pages[0].shape[0]
    grp = 2 * heads
    nq = grp * tdec
    tshift = tdec.bit_length() - 1

    @pl.when(s_id == 0)
    def _():
        m_ref[...] = jnp.full(m_ref.shape, -jnp.inf, F32)
        l_ref[...] = jnp.zeros(l_ref.shape, F32)
        acc_ref[...] = jnp.zeros(acc_ref.shape, F32)
        r = lax.broadcasted_iota(jnp.int32, bias_ref.shape, 0)
        c = lax.broadcasted_iota(jnp.int32, bias_ref.shape, 1)
        bias_ref[...] = jnp.where((c & (grp - 1)) == (r >> tshift), 0.0, -jnp.inf)

    def swap_halves(v):
        n = v.shape[0]
        return pltpu.roll(v.reshape(n // grp, grp, dh), heads, axis=1).reshape(n, dh)

    qs = (q_ref[...] * scale).astype(BF16)

    def partial_softmax(s, vcat):
        m_g = jnp.max(s, axis=-1, keepdims=True)
        p = jnp.exp(s - m_g)
        return m_g, jnp.sum(p, axis=-1, keepdims=True), _dot(p.astype(BF16), vcat)

    def merge(parts):
        m_prev = m_ref[...]
        m_new = m_prev
        for m_g, _, _ in parts:
            m_new = jnp.maximum(m_new, m_g)
        alpha = jnp.exp(m_prev - m_new)
        l_new = alpha * l_ref[...]
        acc_new = alpha * acc_ref[...]
        for m_g, l_g, acc_g in parts:
            w_g = jnp.exp(m_g - m_new)
            l_new = l_new + w_g * l_g
            acc_new = acc_new + w_g * acc_g
        l_ref[...] = l_new
        acc_ref[...] = acc_new
        m_ref[...] = m_new

    ppg = npg // DEC_KEY_GROUPS
    bias = bias_ref[...]
    parts = []
    for g in range(DEC_KEY_GROUPS):
        for pg in range(g * ppg, (g + 1) * ppg):
            rows = slice(pg * pr, (pg + 1) * pr)
            a_ref[rows, :] = k_pages[pg][...].astype(BF16)
            vpg = v_pages[pg][...]
            b_ref[rows, 0:dh] = vpg.astype(BF16)
            b_ref[rows, dh:2 * dh] = swap_halves(vpg).astype(BF16)
        rows = slice(g * ppg * pr, (g + 1) * ppg * pr)
        parts.append(partial_softmax(_dot_nt(qs, a_ref[rows, :]) + bias, b_ref[rows, :]))
    merge(parts)

    @pl.when(s_id == n_steps - 1)
    def _():
        nk = kn_ref.shape[0]
        r = lax.broadcasted_iota(jnp.int32, (nq, nk), 0)
        c = lax.broadcasted_iota(jnp.int32, (nq, nk), 1)
        ok = ((c & (grp - 1)) == (r >> tshift)) & ((c >> (grp.bit_length() - 1)) <= (r & (tdec - 1)))
        s_new = jnp.where(ok, _dot_nt(qs, kn_ref[...].astype(BF16)), -jnp.inf)
        vn = vn_ref[...]
        merge([partial_softmax(s_new, jnp.concatenate([vn, swap_halves(vn)], axis=1).astype(BF16))])
        o = acc_ref[...] / l_ref[...]
        half = nq // 2
        o1 = o[:half]
        o2 = jnp.concatenate([o[half:, dh:], o[half:, :dh]], axis=1)
        lam = _diff_lambda(lam_ref, lam_init)
        o_ref[...] = _diff_finish(o1, o2, lam, sub_ref[...], lam_init).astype(o_ref.dtype)


def _slab_rows(x, batch, tdec, heads):
    dh = HEAD_DIM
    return x.reshape(batch, tdec, heads, 2, dh).transpose(0, 1, 3, 2, 4).reshape(batch, tdec * 2 * heads, dh)


def _dec_attn_call(page_table, lam_vecs, sub, q_rot, k_new, v_new, cache_k, cache_v, layer, tdec, lam_init):
    batch, n_pages = page_table.shape
    n_pool, depth, page, heads, w = cache_k.shape
    dh = HEAD_DIM
    grp = 2 * heads
    assert tdec & (tdec - 1) == 0 and n_pages % PAGES_PER_STEP == 0 and grp == SUBLANES and w == 2 * dh
    n_steps = n_pages // PAGES_PER_STEP
    pr = page * grp
    nq = grp * tdec
    nk_pad = LANES
    kern = functools.partial(_dec_attn_kernel, n_steps=n_steps, heads=heads, tdec=tdec, lam_init=lam_init)

    def slab_view(cache):
        return cache.reshape(n_pool, depth, page, heads, 2, dh).transpose(0, 1, 2, 4, 3, 5).reshape(n_pool, depth, pr, dh)

    q_slab = q_rot.reshape(batch, tdec, heads, 2, dh).transpose(0, 3, 2, 1, 4).reshape(batch, nq, dh)
    pad = ((0, 0), (0, nk_pad - nq), (0, 0))
    kn_slab = jnp.pad(_slab_rows(k_new, batch, tdec, heads), pad)
    vn_slab = jnp.pad(_slab_rows(v_new, batch, tdec, heads), pad)

    def page_spec(pg):
        return pl.BlockSpec((None, None, pr, dh), lambda b, s, pt: (pt[b, s * PAGES_PER_STEP + pg], layer, 0, 0))

    grid_spec = pltpu.PrefetchScalarGridSpec(
        num_scalar_prefetch=1,
        grid=(batch, n_steps),
        in_specs=[pl.BlockSpec((4, dh), lambda b, s, pt: (0, 0)),
                  pl.BlockSpec((1, w), lambda b, s, pt: (0, 0)),
                  pl.BlockSpec((None, nq, dh), lambda b, s, pt: (b, 0, 0)),
                  pl.BlockSpec((None, nk_pad, dh), lambda b, s, pt: (b, 0, 0)),
                  pl.BlockSpec((None, nk_pad, dh), lambda b, s, pt: (b, 0, 0))]
                 + [page_spec(pg) for pg in range(PAGES_PER_STEP)] * 2,
        out_specs=pl.BlockSpec((None, nq // 2, w), lambda b, s, pt: (b, 0, 0)),
        scratch_shapes=[pltpu.VMEM((PAGES_PER_STEP * pr, dh), BF16), pltpu.VMEM((PAGES_PER_STEP * pr, w), BF16),
                        pltpu.VMEM((nq, PAGES_PER_STEP // DEC_KEY_GROUPS * pr), F32),
                        pltpu.VMEM((nq, 1), F32), pltpu.VMEM((nq, 1), F32), pltpu.VMEM((nq, w), F32)],
    )
    o = pl.pallas_call(
        kern,
        out_shape=SDS((batch, nq // 2, w), BF16),
        grid_spec=grid_spec,
        compiler_params=_cp(("arbitrary", "arbitrary")),
        name="diff_attn_decode",
    )(page_table, lam_vecs, sub, q_slab, kn_slab, vn_slab,
      *([slab_view(cache_k)] * PAGES_PER_STEP), *([slab_view(cache_v)] * PAGES_PER_STEP))
    return o.reshape(batch, heads, tdec, w).transpose(0, 2, 1, 3).reshape(batch * tdec, heads * w)


def _causal_conv(xp_ref, x, prev, buf0, is_first, w, tin):
    @pl.when(is_first)
    def _():
        xp_ref[0:SUBLANES, :] = buf0

    @pl.when(jnp.logical_not(is_first))
    def _():
        xp_ref[0:SUBLANES, :] = prev

    xp_ref[SUBLANES:SUBLANES + tin, :] = x
    base = SUBLANES - (CONV_W - 1)
    y = xp_ref[pl.ds(base, tin), :] * w[0:1]
    for j in range(1, CONV_W):
        y = y + xp_ref[pl.ds(base + j, tin), :] * w[j:j + 1]
    return y


def _l2norm(x):
    return x * lax.rsqrt(jnp.sum(x * x, axis=-1, keepdims=True) + EPS)


def _bmm(a, b):
    return jnp.einsum('bij,bjk->bik', a, b, preferred_element_type=F32)


def _bmm_nt(a, b):
    return jnp.einsum('bik,bjk->bij', a, b, preferred_element_type=F32)


def _gdn_chunk_math(y, g, beta, u_ref, w_ref, qt_ref, kt_ref, qk_ref, eg_ref, *, heads, n_levels):
    nb, c_len, _ = y.shape
    dh = HEAD_DIM
    gw = heads * dh
    r = lax.broadcasted_iota(jnp.int32, (c_len, c_len), 0)
    c = lax.broadcasted_iota(jnp.int32, (c_len, c_len), 1)
    incl = r >= c
    strict = r > c
    tri = incl.astype(F32)
    gcs = [jnp.dot(tri, g[b], precision=lax.Precision.HIGHEST, preferred_element_type=F32) for b in range(nb)]
    gc = jnp.stack(gcs)
    gc_t = jnp.stack([x.T for x in gcs])
    level_masks = [((r >> sh) == (c >> sh) + 1) & ((r >> (sh + 1)) == (c >> (sh + 1))) for sh in range(n_levels)]

    for h in range(heads):
        qh = _l2norm(y[:, :, h * dh:(h + 1) * dh]) * (dh ** -0.5)
        kh = _l2norm(y[:, :, gw + h * dh:gw + (h + 1) * dh])
        vh = y[:, :, 2 * gw + h * dh:2 * gw + (h + 1) * dh]
        gcol = gc[:, :, heads + h:heads + h + 1]
        grow = gc_t[:, heads + h:heads + h + 1, :]
        bcol = beta[:, :, h:h + 1]
        glast = gc[:, c_len - 1:c_len, heads + h:heads + h + 1]
        dmat = jnp.where(incl, jnp.exp(jnp.minimum(gcol - grow, 0.0)), 0.0)
        kb = kh * bcol
        kh16 = kh.astype(BF16)
        lmat = jnp.where(strict, _bmm_nt(kb.astype(BF16), kh16) * dmat, 0.0)

        ymat = None
        for sh in range(n_levels):
            e = jnp.where(level_masks[sh], lmat, 0.0)
            if ymat is None:
                ymat = -e
            else:
                y16 = ymat.astype(BF16)
                f = e + _bmm(y16, e.astype(BF16))
                ymat = ymat - (f + _bmm(f.astype(BF16), y16))

        egc = jnp.exp(gcol)
        rhs = jnp.concatenate([vh * bcol, kb * egc], axis=2)
        sol = rhs + _bmm(ymat.astype(BF16), rhs.astype(BF16))
        lanes = slice(h * dh, (h + 1) * dh)
        u_ref[:, :, lanes] = sol[:, :, :dh]
        w_ref[:, :, lanes] = sol[:, :, dh:]
        qt_ref[:, :, lanes] = qh * egc
        kt_ref[:, :, lanes] = kh * jnp.exp(glast - gcol)
        qk_ref[:, :, lanes] = jnp.where(incl, _bmm_nt(qh.astype(BF16), kh16) * dmat, 0.0)
        eg_ref[:, :, lanes] = jnp.broadcast_to(jnp.exp(glast), (nb, SUBLANES, dh))


def _gdn_gates(blk, avec, dtvec):
    beta = jax.nn.sigmoid(blk)
    g = -jnp.exp(avec) * jax.nn.softplus(blk + dtvec)
    return g, beta


def _gdn_prep_seq_kernel(x_ref, prev_ref, buf0_ref, cw_ref, gba_ref, avec_ref, dtvec_ref, gz_ref,
                         u_ref, w_ref, qt_ref, kt_ref, qk_ref, eg_ref, gzo_ref, xp_ref, *, nb, tiles_per_seq, heads):
    c_len = GDN_CHUNK
    tin = nb * c_len
    gzo_ref[...] = gz_ref[...].reshape(gzo_ref.shape)
    is_first = (pl.program_id(0) % tiles_per_seq) == 0
    y = _silu(_causal_conv(xp_ref, x_ref[...], prev_ref[...], buf0_ref[...], is_first, cw_ref[...], tin))
    g, beta = _gdn_gates(gba_ref[...], avec_ref[...], dtvec_ref[...])
    _gdn_chunk_math(y.reshape(nb, c_len, y.shape[1]), g.reshape(nb, c_len, LANES), beta.reshape(nb, c_len, LANES),
                    u_ref, w_ref, qt_ref, kt_ref, qk_ref, eg_ref, heads=heads, n_levels=c_len.bit_length() - 1)


def _gdn_prep_short_kernel(x_ref, buf0_ref, cw_ref, gba_ref, avec_ref, dtvec_ref, gz_ref,
                           u_ref, w_ref, qt_ref, kt_ref, qk_ref, eg_ref, gzo_ref, xp_ref, ypad_ref, gpad_ref, bpad_ref,
                           *, nb, seq, heads):
    cw = x_ref.shape[1]
    gzo_ref[...] = jnp.zeros(gzo_ref.shape, F32)
    gzo_ref[:, 0:seq, :] = gz_ref[...].reshape(nb, seq, gzo_ref.shape[2])
    xp_ref[:, 0:SUBLANES, :] = buf0_ref[...].reshape(nb, SUBLANES, cw)
    xp_ref[:, SUBLANES:SUBLANES + seq, :] = x_ref[...].reshape(nb, seq, cw)
    base = SUBLANES - (CONV_W - 1)
    w = cw_ref[...]
    y = xp_ref[:, pl.ds(base, seq), :] * w[0:1]
    for j in range(1, CONV_W):
        y = y + xp_ref[:, pl.ds(base + j, seq), :] * w[j:j + 1]
    y = _silu(y)
    g, beta = _gdn_gates(gba_ref[...], avec_ref[...], dtvec_ref[...])
    ypad_ref[...] = jnp.zeros(ypad_ref.shape, F32)
    gpad_ref[...] = jnp.zeros(gpad_ref.shape, F32)
    bpad_ref[...] = jnp.zeros(bpad_ref.shape, F32)
    ypad_ref[:, 0:seq, :] = y
    gpad_ref[:, 0:seq, :] = g.reshape(nb, seq, LANES)
    bpad_ref[:, 0:seq, :] = beta.reshape(nb, seq, LANES)
    _gdn_chunk_math(ypad_ref[...], gpad_ref[...], bpad_ref[...], u_ref, w_ref, qt_ref, kt_ref, qk_ref, eg_ref,
                    heads=heads, n_levels=seq.bit_length() - 1)


def _gdn_prep_call(z, row0, batch, seq, buf0, conv_w, avec, dtvec, heads, nb):
    c_len = GDN_CHUNK
    gw = heads * HEAD_DIM
    cw = 3 * gw
    gba_blk = GBA_COL // LANES
    vec = pl.BlockSpec((1, LANES), lambda i: (0, 0))
    if seq >= c_len:
        tin = nb * c_len
        assert seq % tin == 0 and row0 % tin == 0
        tiles_per_seq = seq // tin
        n_tiles = batch * tiles_per_seq
        blk0 = row0 // tin
        prev0 = row0 // SUBLANES
        kern = functools.partial(_gdn_prep_seq_kernel, nb=nb, tiles_per_seq=tiles_per_seq, heads=heads)
        in_specs = [pl.BlockSpec((tin, cw), lambda i: (blk0 + i, 0)),
                    pl.BlockSpec((SUBLANES, cw), lambda i: (jnp.maximum(prev0 + i * (tin // SUBLANES) - 1, 0), 0)),
                    pl.BlockSpec((SUBLANES, cw), lambda i: (i // tiles_per_seq, 0)),
                    pl.BlockSpec((CONV_W, cw), lambda i: (0, 0)),
                    pl.BlockSpec((tin, LANES), lambda i: (blk0 + i, gba_blk)), vec, vec,
                    pl.BlockSpec((tin, gw), lambda i: (blk0 + i, COL_GZ // gw))]
        scratch = [pltpu.VMEM((tin + SUBLANES, cw), F32)]
        args = (z, z, buf0, conv_w, z, avec, dtvec, z)
    else:
        assert seq == SUBLANES and batch % nb == 0 and row0 % (nb * seq) == 0
        tin = nb * seq
        n_tiles = batch // nb
        blk0 = row0 // tin
        kern = functools.partial(_gdn_prep_short_kernel, nb=nb, seq=seq, heads=heads)
        in_specs = [pl.BlockSpec((tin, cw), lambda i: (blk0 + i, 0)),
                    pl.BlockSpec((nb * SUBLANES, cw), lambda i: (i, 0)),
                    pl.BlockSpec((CONV_W, cw), lambda i: (0, 0)),
                    pl.BlockSpec((tin, LANES), lambda i: (blk0 + i, gba_blk)), vec, vec,
                    pl.BlockSpec((tin, gw), lambda i: (blk0 + i, COL_GZ // gw))]
        scratch = [pltpu.VMEM((nb, SUBLANES + seq, cw), F32), pltpu.VMEM((nb, c_len, cw), F32),
                   pltpu.VMEM((nb, c_len, LANES), F32), pltpu.VMEM((nb, c_len, LANES), F32)]
        args = (z, buf0, conv_w, z, avec, dtvec, z)
    n_chunks = n_tiles * nb
    out_blk = pl.BlockSpec((nb, c_len, gw), lambda i: (i, 0, 0))
    return pl.pallas_call(
        kern,
        out_shape=(tuple(SDS((n_chunks, c_len, gw), F32) for _ in range(5)) + (SDS((n_chunks, SUBLANES, gw), F32),)
                   + (SDS((n_chunks, c_len, gw), F32),)),
        grid=(n_tiles,),
        in_specs=in_specs,
        out_specs=(out_blk,) * 5 + (pl.BlockSpec((nb, SUBLANES, gw), lambda i: (i, 0, 0)), out_blk),
        scratch_shapes=scratch,
        compiler_params=_cp(("arbitrary",)),
        name="gdn_prep",
    )(*args)


def _gdn_scan_kernel(u_ref, w_ref, qt_ref, kt_ref, qk_ref, eg_ref, s0_ref, gz_ref, gn_ref, o_ref, s_out_ref, s_ref,
                     *, n_c, tin, heads, nb):
    c_len = GDN_CHUNK
    dh = HEAD_DIM
    ci = pl.program_id(1)

    @pl.when(ci == 0)
    def _():
        s_ref[...] = s0_ref[...]

    gn = gn_ref[...]
    for h in range(heads):
        lanes = slice(h * dh, (h + 1) * dh)
        s_h = s_ref[:, h]
        s16 = s_h.astype(BF16)
        wq = jnp.concatenate([w_ref[:, :, lanes], qt_ref[:, :, lanes]], axis=1).astype(BF16)
        ws_qs = jnp.einsum('bck,bkv->bcv', wq, s16, preferred_element_type=F32)
        v_new = u_ref[:, :, lanes] - ws_qs[:, :c_len]
        v16 = v_new.astype(BF16)
        o = ws_qs[:, c_len:] + jnp.einsum('bij,bjv->biv', qk_ref[:, :, lanes].astype(BF16), v16,
                                          preferred_element_type=F32)
        eg = eg_ref[:, 0:1, lanes]
        s_ref[:, h] = s_h * eg + jnp.einsum('bck,bcv->bkv', kt_ref[:, :, lanes].astype(BF16), v16,
                                            preferred_element_type=F32)
        o = o[:, :tin]
        o = o * lax.rsqrt(jnp.mean(o * o, axis=-1, keepdims=True) + EPS) * gn
        o_ref[:, :, lanes] = (o * _silu(gz_ref[:, 0:tin, lanes])).astype(o_ref.dtype)

    @pl.when(ci == n_c - 1)
    def _():
        s_out_ref[...] = s_ref[...]


def _gdn_scan_call(prep, s0, gn, batch, seq, heads, nb):
    u, w, qt, kt, qk, eg, gz = prep
    c_len = GDN_CHUNK
    dh = HEAD_DIM
    gw = heads * dh
    tin = min(seq, c_len)
    n_c = max(seq // c_len, 1)
    kern = functools.partial(_gdn_scan_kernel, n_c=n_c, tin=tin, heads=heads, nb=nb)
    blk = pl.BlockSpec((nb, c_len, gw), lambda b, i: (b, i, 0))
    return pl.pallas_call(
        kern,
        out_shape=(SDS((batch, seq, gw), BF16), SDS((batch, heads, dh, dh), F32)),
        grid=(batch // nb, n_c),
        in_specs=[blk, blk, blk, blk, blk,
                  pl.BlockSpec((nb, SUBLANES, gw), lambda b, i: (b, i, 0)),
                  pl.BlockSpec((nb, heads, dh, dh), lambda b, i: (b, 0, 0, 0)),
                  blk,
                  pl.BlockSpec((1, dh), lambda b, i: (0, 0))],
        out_specs=(pl.BlockSpec((nb, tin, gw), lambda b, i: (b, i, 0)),
                   pl.BlockSpec((nb, heads, dh, dh), lambda b, i: (b, 0, 0, 0))),
        scratch_shapes=[pltpu.VMEM((nb, heads, dh, dh), F32)],
        compiler_params=_cp(("arbitrary", "arbitrary")),
        name="gdn_scan",
    )(u, w, qt, kt, qk, eg, s0, gz, gn)


def _lru_kernel(x_ref, prev_ref, buf0_ref, cw_ref, cb_ref, wa_ref, ba_ref, wx_ref, bx_ref, lam_ref, h0_ref, lg_ref,
                o_ref, hl_ref, xp_ref, hc_ref, *, tin):
    i = pl.program_id(1)

    @pl.when(i == 0)
    def _():
        hc_ref[...] = h0_ref[...]

    xc = _causal_conv(xp_ref, x_ref[...], prev_ref[...], buf0_ref[...], i == 0, cw_ref[...], tin) + cb_ref[...]
    x16 = xc.astype(BF16)
    rg = jax.nn.sigmoid(_dot(x16, wa_ref[...]) + ba_ref[...])
    ig = jax.nn.sigmoid(_dot(x16, wx_ref[...]) + bx_ref[...])
    log_a = -LRU_C * rg * jax.nn.softplus(-lam_ref[...])
    a = jnp.exp(log_a)
    t = jnp.tanh(log_a)
    b = jnp.sqrt(-2.0 * t / (1.0 - t)) * (ig * xc)

    row = lax.broadcasted_iota(jnp.int32, a.shape, 0)
    d = 1
    while d < tin:
        keep = row >= d
        a_s = jnp.where(keep, pltpu.roll(a, d, 0), 1.0)
        b_s = jnp.where(keep, pltpu.roll(b, d, 0), 0.0)
        b = a * b_s + b
        a = a * a_s
        d *= 2
    hseq = b + a * hc_ref[...]
    hc_ref[...] = hseq[tin - 1:tin, :]
    hl_ref[...] = hseq[tin - 1:tin, :]
    o_ref[...] = (hseq * jax.nn.gelu(lg_ref[...])).astype(o_ref.dtype)


def _lru_call(z, row0, batch, seq, buf0, conv_w, conv_b, wa, ba, wx, bx, lam, h0, col_x, col_g, tin):
    w = conv_w.shape[1]
    n_t = seq // tin
    assert seq == n_t * tin and row0 % tin == 0 and tin & (tin - 1) == 0
    blk0 = row0 // tin
    prev_per_tile = tin // SUBLANES
    prev0 = row0 // SUBLANES
    kern = functools.partial(_lru_kernel, tin=tin)
    vec = pl.BlockSpec((1, w), lambda b, i: (0, 0))
    return pl.pallas_call(
        kern,
        out_shape=(SDS((batch, seq, w), BF16), SDS((batch, 1, w), F32)),
        grid=(batch, n_t),
        in_specs=[pl.BlockSpec((tin, w), lambda b, i: (blk0 + b * n_t + i, col_x // w)),
                  pl.BlockSpec((SUBLANES, w),
                               lambda b, i: (jnp.maximum(prev0 + (b * n_t + i) * prev_per_tile - 1, 0), col_x // w)),
                  pl.BlockSpec((SUBLANES, w), lambda b, i: (b, 0)),
                  pl.BlockSpec((CONV_W, w), lambda b, i: (0, 0)),
                  vec,
                  pl.BlockSpec((w, w), lambda b, i: (0, 0)), vec,
                  pl.BlockSpec((w, w), lambda b, i: (0, 0)), vec,
                  vec,
                  pl.BlockSpec((None, 1, w), lambda b, i: (b, 0, 0)),
                  pl.BlockSpec((tin, w), lambda b, i: (blk0 + b * n_t + i, col_g // w))],
        out_specs=(pl.BlockSpec((None, tin, w), lambda b, i: (b, i, 0)),
                   pl.BlockSpec((None, 1, w), lambda b, i: (b, 0, 0))),
        scratch_shapes=[pltpu.VMEM((tin + SUBLANES, w), F32), pltpu.VMEM((1, w), F32)],
        compiler_params=_cp(("arbitrary", "arbitrary")),
        name="rglru",
    )(z, z, buf0, conv_w, conv_b, wa, ba, wx, bx, lam, h0, z)


GDN_W = 512
DIFF_W = 1024
LRU_W = 512
COL_GQKV = 0
COL_GZ = 3 * GDN_W
COL_DQ = COL_GZ + GDN_W
COL_DK = COL_DQ + DIFF_W
COL_DV = COL_DK + DIFF_W
COL_LX = COL_DV + DIFF_W
COL_LG = COL_LX + LRU_W
GBA_COL = COL_LG + LRU_W
NZ = 6400


def _pad_state_rows(buf):
    b, k, c = buf.shape
    return jnp.pad(buf, ((0, 0), (SUBLANES - k, 0), (0, 0))).reshape(b * SUBLANES, c)


def kernel(x_prompt, x_sample, cache_k, cache_v, state_gdn, state_gdn_conv, state_lru, state_lru_conv, page_table,
           c_prompt, c_sample, ada_w, ada_b, norm_ffn1, ffn1_w_gate, ffn1_w_up, ffn1_w_down, norm_mix, w_in,
           gdn_conv_w, gdn_a_log, gdn_dt_bias, gdn_norm, diff_lq1, diff_lk1, diff_lq2, diff_lk2, diff_subln,
           lru_conv_w, lru_conv_b, lru_wa, lru_ba, lru_wx, lru_bx, lru_lambda, w_out, norm_ffn2, ffn2_w_gate,
           ffn2_w_up, ffn2_w_down, final_norm):
    bp, tp, d = x_prompt.shape
    bd, td, _ = x_sample.shape
    depth = ada_w.shape[0]
    gdn_heads = gdn_a_log.shape[1]
    diff_heads = cache_k.shape[3]
    page = cache_k.shape[2]
    past_len = page_table.shape[1] * page
    mp_rows = bp * tp
    m = mp_rows + bd * td
    tm = math.gcd(m, ROW_TILE)
    groups_per_seq = tp // SUBLANES
    assert td == SUBLANES and gdn_heads * HEAD_DIM == GDN_W and d == 2048 and groups_per_seq & (groups_per_seq - 1) == 0
    rmap = _RowMap(groups_prompt=mp_rows // SUBLANES, shift_prompt=groups_per_seq.bit_length() - 1, n_prompt=bp)

    x = jnp.concatenate([x_prompt.reshape(mp_rows, d), x_sample.reshape(bd * td, d)], axis=0)
    n_c = bp + bd
    c_all = jnp.pad(jnp.concatenate([c_prompt, c_sample], axis=0), ((0, (-n_c) % LANES), (0, 0)))
    mod = _ada_call(c_all, ada_w, ada_b)

    zeros_gdn_buf = jnp.zeros((bp * SUBLANES, 3 * GDN_W), F32)
    zeros_lru_buf = jnp.zeros((bp * SUBLANES, LRU_W), F32)
    tail = CONV_W - 1

    def last_rows(zz, row0, bsz, seq, col, width):
        if seq <= SUBLANES:
            return zz[row0:row0 + bsz * seq, col:col + width].reshape(bsz, seq, width)[:, seq - tail:]
        return jnp.stack([zz[row0 + (b + 1) * seq - tail:row0 + (b + 1) * seq, col:col + width] for b in range(bsz)])

    outs = {k: [] for k in ("k_p", "v_p", "s_p", "sb_p", "h_p", "hb_p", "k_s", "v_s", "s_s", "sb_s", "h_s", "hb_s")}
    prompt_kv = None
    for l in range(depth):
        lam_init = 0.8 - 0.6 * math.exp(-0.3 * l)

        x = _ffn_call(x, norm_ffn1[l][None], mod, l, 0, ffn1_w_gate, ffn1_w_up, ffn1_w_down, rmap,
                      tm=tm, tf=512, tf_first=256)

        wl = w_in[l]
        o1 = 4 * GDN_W
        o2 = o1 + 2 * gdn_heads
        w_re = jnp.concatenate([wl[:, :o1], wl[:, o2:], wl[:, o1:o2]], axis=1)
        w_re = jnp.pad(w_re, ((0, 0), (0, NZ - w_re.shape[1]))).astype(BF16)
        z = _proj_in_call(x, norm_mix[l][None], mod, l, 3, w_re, rmap, tm=tm, tn=1280)

        lam_vecs = jnp.stack([diff_lq1[l], diff_lk1[l], diff_lq2[l], diff_lk2[l]])
        sub = diff_subln[l][None]
        avec = jnp.zeros((1, LANES), F32).at[0, gdn_heads:2 * gdn_heads].set(gdn_a_log[l])
        dtvec = jnp.zeros((1, LANES), F32).at[0, gdn_heads:2 * gdn_heads].set(gdn_dt_bias[l])
        bw = lru_wa.shape[2]
        eye = jnp.eye(LRU_BLOCKS, dtype=F32)
        wa_full = (eye[:, None, :, None] * lru_wa[l][:, :, None, :]).reshape(LRU_W, LRU_W).astype(BF16)
        wx_full = (eye[:, None, :, None] * lru_wx[l][:, :, None, :]).reshape(LRU_W, LRU_W).astype(BF16)

        mixes = {}
        for path in ("p", "s"):
            if path == "p":
                row0, bsz, seq, pos0 = 0, bp, tp, 0
                gdn_buf0, lru_buf0 = zeros_gdn_buf, zeros_lru_buf
                s0 = jnp.zeros((bp, gdn_heads, HEAD_DIM, HEAD_DIM), F32)
                h0 = jnp.zeros((bp, 1, LRU_W), F32)
            else:
                row0, bsz, seq, pos0 = mp_rows, bd, td, past_len
                gdn_buf0 = _pad_state_rows(state_gdn_conv[:, l])
                lru_buf0 = _pad_state_rows(state_lru_conv[:, l])
                s0 = state_gdn[:, l]
                h0 = state_lru[:, l][:, None, :]
            rows = bsz * seq

            prep = _gdn_prep_call(z, row0, bsz, seq, gdn_buf0, gdn_conv_w[l], avec, dtvec, gdn_heads,
                                  nb=math.gcd(4, seq // GDN_CHUNK) if path == "p" else math.gcd(bsz, 8))
            prep3 = tuple(p.reshape(bsz, -1, GDN_W) for p in prep)
            o_gdn, s_new = _gdn_scan_call(prep3, s0, gdn_norm[l][None], bsz, seq, gdn_heads, nb=math.gcd(bsz, 4))

            if path == "p":
                q_rot, k_rot, *prompt_kv = _rope_call(z, row0, rows, seq, pos0, COL_DQ, COL_DK, COL_DV, DIFF_W,
                                                      tq=math.gcd(seq, 256), slab=(l, depth, diff_heads, prompt_kv))
                o_diff = _flash_call(lam_vecs, sub, q_rot, k_rot, z, COL_DV, bsz, seq, diff_heads, lam_init,
                                     tq=math.gcd(seq, 512))
            else:
                q_rot, k_rot, v_rows = _rope_call(z, row0, rows, seq, pos0, COL_DQ, COL_DK, COL_DV, DIFF_W,
                                                  tq=math.gcd(rows, 256))
                o_diff = _dec_attn_call(page_table, lam_vecs, sub, q_rot, k_rot, v_rows, cache_k, cache_v, l, seq,
                                        lam_init)
                outs["k_s"].append(k_rot.reshape(bsz, seq, diff_heads, 2 * HEAD_DIM))
                outs["v_s"].append(v_rows.reshape(bsz, seq, diff_heads, 2 * HEAD_DIM))

            o_lru, h_last = _lru_call(z, row0, bsz, seq, lru_buf0, lru_conv_w[l], lru_conv_b[l][None], wa_full,
                                      lru_ba[l][None], wx_full, lru_bx[l][None], lru_lambda[l][None], h0,
                                      COL_LX, COL_LG, tin=min(seq, 256))

            mixes[path] = (o_gdn.reshape(rows, GDN_W), o_diff.reshape(rows, DIFF_W), o_lru.reshape(rows, LRU_W))
            outs["s_" + path].append(s_new)
            outs["sb_" + path].append(last_rows(z, row0, bsz, seq, COL_GQKV, 3 * GDN_W))
            outs["h_" + path].append(h_last.reshape(bsz, LRU_W))
            outs["hb_" + path].append(last_rows(z, row0, bsz, seq, COL_LX, LRU_W))

        x = _proj_out_call(mixes["p"], mixes["s"], w_out, l, x, mod, 5, rmap,
                           tm=math.gcd(math.gcd(mp_rows, bd * td), 256))
        x = _ffn_call(x, norm_ffn2[l][None], mod, l, 6, ffn2_w_gate, ffn2_w_up, ffn2_w_down, rmap,
                      tm=tm, tf=512, tf_first=256)

    y_prompt = _final_norm_call(x, final_norm[None], 0, mp_rows, math.gcd(mp_rows, 256)).reshape(bp, tp, d)
    y_sample = _final_norm_call(x, final_norm[None], mp_rows, bd * td, math.gcd(bd * td, 256)).reshape(bd, td, d)
    st = {k: jnp.stack(v, axis=1) for k, v in outs.items() if v}

    def from_slab(s):
        return (s.reshape(bp, depth, tp, 2, diff_heads, HEAD_DIM).transpose(0, 1, 2, 4, 3, 5)
                .reshape(bp, depth, tp, diff_heads, 2 * HEAD_DIM))

    st["k_p"], st["v_p"] = (from_slab(s) for s in prompt_kv)
    return (y_prompt, y_sample, st["k_p"], st["v_p"], st["s_p"], st["sb_p"], st["h_p"], st["hb_p"],
            st["k_s"], st["v_s"], st["s_s"], st["sb_s"], st["h_s"], st["hb_s"])
```
